```python
import math
import jax, jax.numpy as jnp
from jax import lax
import numpy as np

D_MODEL = 2048
BATCH = 4
SEQ = 2048
DEPTH = 4
DEC_BATCH = 8
DEC_SEQ = 8
PAST_LEN = 16384
PAGE_SIZE = 128

N_MIXERS = 3
KIND_POOL = 0
KIND_SSM = 1
KIND_SB = 2
LAYER_KINDS = tuple(i % N_MIXERS for i in range(DEPTH))
N_POOL_LAYERS = LAYER_KINDS.count(KIND_POOL)
N_SSM_LAYERS = LAYER_KINDS.count(KIND_SSM)
N_SB_LAYERS = LAYER_KINDS.count(KIND_SB)

POOL_WINDOWS = (2, 4, 8, 16)
POOL_GROUPS = len(POOL_WINDOWS)
POOL_GROUP_DIM = D_MODEL // POOL_GROUPS
POOL_BUF = max(POOL_WINDOWS) - 1

S5_GROUP_SIZE = 16
S5_GROUPS = D_MODEL // S5_GROUP_SIZE
S5_STATE = 64

SB_HEAD_DIM = 128
SB_HEADS = D_MODEL // SB_HEAD_DIM
BLOCK_Q = 128
SB_BIAS_LO = -8.0
SB_BIAS_HI = -4.0

D_FF = -(-8 * D_MODEL // (3 * 256)) * 256

RMS_EPS = 1e-6

kernel_name = 'hybrid_pool_s5_stickbreaking_decoder_step'


def rmsnorm(x, g):
    x32 = x.astype(jnp.float32)
    y = x32 * lax.rsqrt(jnp.mean(x32 * x32, axis=-1, keepdims=True) + RMS_EPS)
    return (y * g.astype(jnp.float32)).astype(x.dtype)


def swiglu_ffn(h, w_gate, w_up, w_down):
    return (jax.nn.silu(h @ w_gate) * (h @ w_up)) @ w_down


def pool_mixer(h, prev, start_pos, w_pool, scale):
    B, T, _ = h.shape
    ext = jnp.concatenate([prev.astype(h.dtype), h], axis=1)
    cs = jnp.concatenate([jnp.zeros((B, 1, D_MODEL), jnp.float32),
                          jnp.cumsum(ext.astype(jnp.float32), axis=1)], axis=1)
    h32 = h.astype(jnp.float32)
    pos = start_pos + jnp.arange(T)
    P = POOL_BUF
    outs = []
    for g, w in enumerate(POOL_WINDOWS):
        sl = slice(g * POOL_GROUP_DIM, (g + 1) * POOL_GROUP_DIM)
        win = cs[:, P + 1:P + 1 + T, sl] - cs[:, P + 1 - w:P + 1 - w + T, sl]
        cnt = jnp.minimum(pos + 1, w).astype(jnp.float32)
        outs.append(win / cnt[None, :, None] - h32[:, :, sl])
    p = jnp.stack(outs, axis=2)
    y = jnp.einsum('btgc,gcd->btgd', p, w_pool.astype(jnp.float32)).reshape(B, T, D_MODEL)
    y = y * scale.astype(jnp.float32)
    return y, ext[:, -P:]


def s5_mixer(h, h0_re, h0_im, a_re, a_im, b_re, b_im, c_re, c_im, d, log_dt, w_glu_a, w_glu_b):
    f32 = jnp.float32
    B, T, _ = h.shape
    u = h.astype(f32)
    ug = u.reshape(B, T, S5_GROUPS, S5_GROUP_SIZE)
    a_re = a_re.astype(f32); a_im = a_im.astype(f32)
    dt = jnp.exp(log_dt.astype(f32))[:, None]
    mag = jnp.exp(dt * a_re)
    abar_re = mag * jnp.cos(dt * a_im)
    abar_im = mag * jnp.sin(dt * a_im)
    den = a_re * a_re + a_im * a_im
    nr = abar_re - 1.0
    f_re = (nr * a_re + abar_im * a_im) / den
    f_im = (abar_im * a_re - nr * a_im) / den
    b_re = b_re.astype(f32); b_im = b_im.astype(f32)
    bbar_re = f_re[..., None] * b_re - f_im[..., None] * b_im
    bbar_im = f_re[..., None] * b_im + f_im[..., None] * b_re
    bu_re = jnp.einsum('btgc,gnc->btgn', ug, bbar_re)
    bu_im = jnp.einsum('btgc,gnc->btgn', ug, bbar_im)
    a_r = jnp.broadcast_to(abar_re, bu_re.shape)
    a_i = jnp.broadcast_to(abar_im, bu_im.shape)

    def combine(e1, e2):
        ar1, ai1, br1, bi1 = e1
        ar2, ai2, br2, bi2 = e2
        return (ar2 * ar1 - ai2 * ai1,
                ar2 * ai1 + ai2 * ar1,
                ar2 * br1 - ai2 * bi1 + br2,
                ar2 * bi1 + ai2 * br1 + bi2)

    pr, pi, sr, si = lax.associative_scan(combine, (a_r, a_i, bu_re, bu_im), axis=1)
    h0r = h0_re.astype(f32)[:, None]
    h0i = h0_im.astype(f32)[:, None]
    sr = sr + pr * h0r - pi * h0i
    si = si + pr * h0i + pi * h0r
    y = (jnp.einsum('btgn,gcn->btgc', sr, c_re.astype(f32))
         - jnp.einsum('btgn,gcn->btgc', si, c_im.astype(f32))).reshape(B, T, D_MODEL)
    y = y + d.astype(f32) * u
    g = jax.nn.gelu(y, approximate=False)
    out = (g @ w_glu_a.astype(f32)) * jax.nn.sigmoid(g @ w_glu_b.astype(f32))
    return out, sr[:, -1], si[:, -1]


def sb_project(h, w_qkv, g_q, g_k):
    B, T, _ = h.shape
    qkv = (h @ w_qkv).reshape(B, T, 3, SB_HEADS, SB_HEAD_DIM)
    q = rmsnorm(qkv[:, :, 0], g_q)
    k = rmsnorm(qkv[:, :, 1], g_k)
    v = qkv[:, :, 2]
    return q, k, v


def stick_breaking(q, k, v, bias, q_pos, k_pos):
    f32 = jnp.float32
    z = jnp.einsum('bqhd,bkhd->bhqk', q.astype(f32), k.astype(f32)) * (1.0 / math.sqrt(SB_HEAD_DIM))
    z = z + bias.astype(f32)[None, :, None, None]
    mask = (k_pos[None, :] < q_pos[:, None])[None, None]
    log_keep = jnp.where(mask, jax.nn.log_sigmoid(-z), 0.0)
    later = lax.cumsum(log_keep, axis=3, reverse=True) - log_keep
    w = jnp.where(mask, jnp.exp(jax.nn.log_sigmoid(z) + later), 0.0)
    return jnp.einsum('bhqk,bkhd->bqhd', w, v.astype(f32))


def sb_prompt(q, k, v, bias):
    T = q.shape[1]
    outs = []
    for i in range(T // BLOCK_Q):
        lo, hi = i * BLOCK_Q, (i + 1) * BLOCK_Q
        outs.append(stick_breaking(q[:, lo:hi], k[:, :hi], v[:, :hi], bias,
                                   jnp.arange(lo, hi), jnp.arange(hi)))
    return jnp.concatenate(outs, axis=1)


def setup_inputs(seed: int = 0) -> dict:
    key = jax.random.key(seed)
    ks = jax.random.split(key, 40)
    f32 = jnp.float32
    n_pages = PAST_LEN // PAGE_SIZE
    n_used = DEC_BATCH * n_pages
    n_phys = n_used + -(-n_used // 4)

    def normal(k, shape, scale):
        return scale * jax.random.normal(k, shape, f32)

    def gain(k, shape):
        return 1.0 + 0.02 * jax.random.normal(k, shape, f32)

    n_idx = jnp.arange(S5_STATE, dtype=f32)
    inputs = {
        'x_prompt': normal(ks[0], (BATCH, SEQ, D_MODEL), 1.0),
        'x_sample': normal(ks[1], (DEC_BATCH, DEC_SEQ, D_MODEL), 1.0),
        'cache_pool': normal(ks[2], (N_POOL_LAYERS, DEC_BATCH, POOL_BUF, D_MODEL), 1.0),
        'state_ssm_re': normal(ks[3], (N_SSM_LAYERS, DEC_BATCH, S5_GROUPS, S5_STATE), 0.1),
        'state_ssm_im': normal(ks[4], (N_SSM_LAYERS, DEC_BATCH, S5_GROUPS, S5_STATE), 0.1),
        'cache_k': normal(ks[5], (N_SB_LAYERS, n_phys, PAGE_SIZE, SB_HEADS, SB_HEAD_DIM), 1.0),
        'cache_v': normal(ks[6], (N_SB_LAYERS, n_phys, PAGE_SIZE, SB_HEADS, SB_HEAD_DIM), 1.0),
        'page_table': jax.random.permutation(ks[7], n_phys)[:n_used].reshape(DEC_BATCH, n_pages).astype(jnp.int32),
        'norm_mix': gain(ks[8], (DEPTH, D_MODEL)),
        'norm_ffn': gain(ks[9], (DEPTH, D_MODEL)),
        'w_ffn_gate': normal(ks[10], (DEPTH, D_MODEL, D_FF), D_MODEL ** -0.5),
        'w_ffn_up': normal(ks[11], (DEPTH, D_MODEL, D_FF), D_MODEL ** -0.5),
        'w_ffn_down': normal(ks[12], (DEPTH, D_FF, D_MODEL), D_FF ** -0.5),
        'w_pool': normal(ks[13], (N_POOL_LAYERS, POOL_GROUPS, POOL_GROUP_DIM, POOL_GROUP_DIM), POOL_GROUP_DIM ** -0.5),
        'pool_scale': 0.5 + 0.05 * jax.random.normal(ks[14], (N_POOL_LAYERS, D_MODEL), f32),
        'ssm_a_re': -0.5 + 0.01 * jax.random.normal(ks[15], (N_SSM_LAYERS, S5_GROUPS, S5_STATE), f32),
        'ssm_a_im': math.pi * n_idx + 0.01 * jax.random.normal(ks[16], (N_SSM_LAYERS, S5_GROUPS, S5_STATE), f32),
        'ssm_b_re': normal(ks[17], (N_SSM_LAYERS, S5_GROUPS, S5_STATE, S5_GROUP_SIZE), (2 * S5_GROUP_SIZE) ** -0.5),
        'ssm_b_im': normal(ks[18], (N_SSM_LAYERS, S5_GROUPS, S5_STATE, S5_GROUP_SIZE), (2 * S5_GROUP_SIZE) ** -0.5),
        'ssm_c_re': normal(ks[19], (N_SSM_LAYERS, S5_GROUPS, S5_GROUP_SIZE, S5_STATE), S5_STATE ** -0.5),
        'ssm_c_im': normal(ks[20], (N_SSM_LAYERS, S5_GROUPS, S5_GROUP_SIZE, S5_STATE), S5_STATE ** -0.5),
        'ssm_d': normal(ks[21], (N_SSM_LAYERS, D_MODEL), 0.5),
        'ssm_log_dt': jax.random.uniform(ks[22], (N_SSM_LAYERS, S5_GROUPS), f32, math.log(1e-3), math.log(1e-1)),
        'w_glu_a': normal(ks[23], (N_SSM_LAYERS, D_MODEL, D_MODEL), D_MODEL ** -0.5),
        'w_glu_b': normal(ks[24], (N_SSM_LAYERS, D_MODEL, D_MODEL), D_MODEL ** -0.5),
        'w_qkv': normal(ks[25], (N_SB_LAYERS, D_MODEL, 3 * D_MODEL), D_MODEL ** -0.5),
        'w_o': normal(ks[26], (N_SB_LAYERS, D_MODEL, D_MODEL), D_MODEL ** -0.5),
        'sb_q_norm': gain(ks[27], (N_SB_LAYERS, SB_HEAD_DIM)),
        'sb_k_norm': gain(ks[28], (N_SB_LAYERS, SB_HEAD_DIM)),
        'sb_bias': jax.random.uniform(ks[29], (N_SB_LAYERS, SB_HEADS), f32, SB_BIAS_LO, SB_BIAS_HI),
    }
    return inputs


def reference(x_prompt, x_sample, cache_pool, state_ssm_re, state_ssm_im, cache_k, cache_v, page_table,
              norm_mix, norm_ffn, w_ffn_gate, w_ffn_up, w_ffn_down, w_pool, pool_scale,
              ssm_a_re, ssm_a_im, ssm_b_re, ssm_b_im, ssm_c_re, ssm_c_im, ssm_d, ssm_log_dt,
              w_glu_a, w_glu_b, w_qkv, w_o, sb_q_norm, sb_k_norm, sb_bias):
    xp, xs = x_prompt, x_sample
    Bp, Tp, _ = xp.shape
    Bs, Ts, _ = xs.shape
    pool_p, pool_s = [], []
    ssm_re_p, ssm_im_p, ssm_re_s, ssm_im_s = [], [], [], []
    k_p, v_p, k_s, v_s = [], [], [], []
    for i in range(DEPTH):
        kind = LAYER_KINDS[i]
        j = i // N_MIXERS
        hp = rmsnorm(xp, norm_mix[i])
        hs = rmsnorm(xs, norm_mix[i])
        if kind == KIND_POOL:
            zero_buf = jnp.zeros((Bp, POOL_BUF, D_MODEL), hp.dtype)
            mp, bp = pool_mixer(hp, zero_buf, 0, w_pool[j], pool_scale[j])
            ms, bs = pool_mixer(hs, cache_pool[j], PAST_LEN, w_pool[j], pool_scale[j])
            pool_p.append(bp)
            pool_s.append(bs)
        elif kind == KIND_SSM:
            params = (ssm_a_re[j], ssm_a_im[j], ssm_b_re[j], ssm_b_im[j], ssm_c_re[j], ssm_c_im[j],
                      ssm_d[j], ssm_log_dt[j], w_glu_a[j], w_glu_b[j])
            zero_state = jnp.zeros((Bp, S5_GROUPS, S5_STATE), jnp.float32)
            mp, sr_p, si_p = s5_mixer(hp, zero_state, zero_state, *params)
            ms, sr_s, si_s = s5_mixer(hs, state_ssm_re[j], state_ssm_im[j], *params)
            ssm_re_p.append(sr_p); ssm_im_p.append(si_p)
            ssm_re_s.append(sr_s); ssm_im_s.append(si_s)
        else:
            q_pr, k_pr, v_pr = sb_project(hp, w_qkv[j], sb_q_norm[j], sb_k_norm[j])
            o_p = sb_prompt(q_pr, k_pr, v_pr, sb_bias[j])
            q_sm, k_sm, v_sm = sb_project(hs, w_qkv[j], sb_q_norm[j], sb_k_norm[j])
            past_k = cache_k[j][page_table].reshape(Bs, -1, SB_HEADS, SB_HEAD_DIM)
            past_v = cache_v[j][page_table].reshape(Bs, -1, SB_HEADS, SB_HEAD_DIM)
            past = past_k.shape[1]
            k_all = jnp.concatenate([past_k.astype(k_sm.dtype), k_sm], axis=1)
            v_all = jnp.concatenate([past_v.astype(v_sm.dtype), v_sm], axis=1)
            o_s = stick_breaking(q_sm, k_all, v_all, sb_bias[j], past + jnp.arange(Ts), jnp.arange(past + Ts))
            mp = o_p.reshape(Bp, Tp, D_MODEL) @ w_o[j].astype(jnp.float32)
            ms = o_s.reshape(Bs, Ts, D_MODEL) @ w_o[j].astype(jnp.float32)
            k_p.append(k_pr); v_p.append(v_pr)
            k_s.append(k_sm); v_s.append(v_sm)
        xp = xp + mp.astype(xp.dtype)
        xs = xs + ms.astype(xs.dtype)
        xp = xp + swiglu_ffn(rmsnorm(xp, norm_ffn[i]), w_ffn_gate[i], w_ffn_up[i], w_ffn_down[i]).astype(xp.dtype)
        xs = xs + swiglu_ffn(rmsnorm(xs, norm_ffn[i]), w_ffn_gate[i], w_ffn_up[i], w_ffn_down[i]).astype(xs.dtype)
    y_prompt = xp
    y_sample = xs
    new_pool_prompt = jnp.stack(pool_p, axis=0)
    new_pool_sample = jnp.stack(pool_s, axis=0)
    new_ssm_re_prompt = jnp.stack(ssm_re_p, axis=0)
    new_ssm_im_prompt = jnp.stack(ssm_im_p, axis=0)
    new_ssm_re_sample = jnp.stack(ssm_re_s, axis=0)
    new_ssm_im_sample = jnp.stack(ssm_im_s, axis=0)
    new_k_prompt = jnp.stack(k_p, axis=0)
    new_v_prompt = jnp.stack(v_p, axis=0)
    new_k_sample = jnp.stack(k_s, axis=0)
    new_v_sample = jnp.stack(v_s, axis=0)
    return (y_prompt, y_sample, new_pool_prompt, new_pool_sample,
            new_ssm_re_prompt, new_ssm_im_prompt, new_ssm_re_sample, new_ssm_im_sample,
            new_k_prompt, new_v_prompt, new_k_sample, new_v_sample)
```

```python
import functools
import math

import jax
import jax.numpy as jnp
from jax import lax
from jax.experimental import pallas as pl
from jax.experimental.pallas import tpu as pltpu

f32 = jnp.float32
bf16 = jnp.bfloat16

RMS_EPS = 1e-6
N_MIXERS = 3
POOL_WINDOWS = (2, 4, 8, 16)
POOL_HALO = 16
S5_GROUP_SIZE = 16
S5_STATE = 64
S5_PACK = 8
SB_HEAD_DIM = 128
PAGE_SIZE = 128
LANES = 128
MIB = 1024 * 1024
VMEM_CAP = 56 * MIB

TM = 512
TM_GLU = 256
POOL_TT = 512
S5_TT = 128
TQ = 128

_NT = (((1,), (1,)), ((), ()))


def _params(sem, vmem_bytes):
    return pltpu.CompilerParams(dimension_semantics=sem,
                                vmem_limit_bytes=int(min(VMEM_CAP, vmem_bytes + 12 * MIB)))


def _rms(x, g):
    return x * lax.rsqrt(jnp.mean(x * x, axis=-1, keepdims=True) + RMS_EPS) * g


def _softplus(z):
    return jnp.maximum(z, 0.0) + jnp.log1p(jnp.exp(-jnp.abs(z)))


def _resident(shape):
    nd = len(shape)
    return pl.BlockSpec(shape, lambda *_: (0,) * nd, pipeline_mode=pl.Buffered(1))


def _ffn_kernel(x_ref, g_ref, wg_ref, wu_ref, wd_ref, o_ref, h_ref):
    f = pl.program_id(1)

    @pl.when(f == 0)
    def _():
        x = x_ref[...]
        h_ref[...] = _rms(x, g_ref[...]).astype(bf16)
        o_ref[...] = x

    h = h_ref[...]
    a = jnp.dot(h, wg_ref[...], preferred_element_type=f32)
    b = jnp.dot(h, wu_ref[...], preferred_element_type=f32)
    act = (a * jax.nn.sigmoid(a) * b).astype(bf16)
    o_ref[...] += jnp.dot(act, wd_ref[...], preferred_element_type=f32)


def _ffn(x, g, wg, wu, wd, *, tm, tf=512):
    m, d = x.shape
    nf = wg.shape[1] // tf
    vmem = 2 * 2 * tm * d * 4 + tm * d * 2 + 2 * 3 * d * tf * 2 + 3 * tm * tf * 4
    return pl.pallas_call(
        _ffn_kernel,
        out_shape=jax.ShapeDtypeStruct((m, d), f32),
        grid=(m // tm, nf),
        in_specs=[pl.BlockSpec((tm, d), lambda i, j: (i, 0)),
                  pl.BlockSpec((1, d), lambda i, j: (0, 0)),
                  pl.BlockSpec((d, tf), lambda i, j: (0, j)),
                  pl.BlockSpec((d, tf), lambda i, j: (0, j)),
                  pl.BlockSpec((tf, d), lambda i, j: (j, 0))],
        out_specs=pl.BlockSpec((tm, d), lambda i, j: (i, 0)),
        scratch_shapes=[pltpu.VMEM((tm, d), bf16)],
        compiler_params=_params(("parallel", "arbitrary"), vmem),
        name="ffn",
    )(x, g.reshape(1, d), wg, wu, wd)


def _proj_kernel(x_ref, a_ref, w_ref, o_ref):
    o_ref[...] = x_ref[...] + jnp.dot(a_ref[...], w_ref[...], preferred_element_type=f32)


def _proj_residual(x, a, w, *, tm):
    m, d = x.shape
    k = a.shape[1]
    vmem = 2 * 2 * tm * d * 4 + 2 * tm * k * 2 + k * d * 2 + tm * d * 4
    return pl.pallas_call(
        _proj_kernel,
        out_shape=jax.ShapeDtypeStruct((m, d), f32),
        grid=(m // tm,),
        in_specs=[pl.BlockSpec((tm, d), lambda i: (i, 0)),
                  pl.BlockSpec((tm, k), lambda i: (i, 0)),
                  _resident((k, d))],
        out_specs=pl.BlockSpec((tm, d), lambda i: (i, 0)),
        compiler_params=_params(("parallel",), vmem),
        name="proj_residual",
    )(x, a, w)


def _glu_kernel(x_ref, a_ref, wa_ref, wb_ref, o_ref):
    a = a_ref[...]
    p = jnp.dot(a, wa_ref[...], preferred_element_type=f32)
    q = jnp.dot(a, wb_ref[...], preferred_element_type=f32)
    o_ref[...] = x_ref[...] + p * jax.nn.sigmoid(q)


def _glu_residual(x, a, wa, wb, *, tm):
    m, d = x.shape
    k = a.shape[1]
    vmem = 2 * 2 * tm * d * 4 + 2 * tm * k * 2 + 2 * k * d * 2 + 3 * tm * d * 4
    return pl.pallas_call(
        _glu_kernel,
        out_shape=jax.ShapeDtypeStruct((m, d), f32),
        grid=(m // tm,),
        in_specs=[pl.BlockSpec((tm, d), lambda i: (i, 0)),
                  pl.BlockSpec((tm, k), lambda i: (i, 0)),
                  _resident((k, d)), _resident((k, d))],
        out_specs=pl.BlockSpec((tm, d), lambda i: (i, 0)),
        compiler_params=_params(("parallel",), vmem),
        name="glu_residual",
    )(x, a, wa, wb)


def _pool_kernel(x_ref, prev_ref, g_ref, w_ref, sc_ref, o_ref, buf_ref, ext_ref, *, tt, start_pos):
    t = pl.program_id(1)
    halo = POOL_HALO

    @pl.when(t == 0)
    def _():
        ext_ref[0:halo, :] = prev_ref[0]

    @pl.when(t > 0)
    def _():
        ext_ref[0:halo, :] = ext_ref[tt:tt + halo, :]

    x = x_ref[0]
    h = _rms(x, g_ref[...])
    ext_ref[halo:halo + tt, :] = h
    buf_ref[0] = ext_ref[tt:tt + halo, :]

    gd = x.shape[1] // len(POOL_WINDOWS)
    pos = start_pos + t * tt + lax.broadcasted_iota(jnp.int32, (tt, 1), 0)
    for gi, w in enumerate(POOL_WINDOWS):
        sl = slice(gi * gd, (gi + 1) * gd)
        win = h[:, sl]
        for i in range(1, w):
            win = win + ext_ref[halo - i:halo - i + tt, sl]
        cnt = jnp.minimum(pos + 1, w).astype(f32)
        p = win / cnt - h[:, sl]
        y = jnp.dot(p.astype(bf16), w_ref[gi], preferred_element_type=f32)
        o_ref[0, :, sl] = x[:, sl] + y * sc_ref[:, sl]


def _pool_layer(x, prev16, g, w_pool, scale, *, tt, start_pos):
    b, t, d = x.shape
    ng, gd = w_pool.shape[0], w_pool.shape[1]
    vmem = 2 * 2 * tt * d * 4 + (tt + POOL_HALO) * d * 4 + ng * gd * gd * 2 + 4 * tt * d * 4
    return pl.pallas_call(
        functools.partial(_pool_kernel, tt=tt, start_pos=start_pos),
        out_shape=(jax.ShapeDtypeStruct((b, t, d), f32), jax.ShapeDtypeStruct((b, POOL_HALO, d), f32)),
        grid=(b, t // tt),
        in_specs=[pl.BlockSpec((1, tt, d), lambda i, j: (i, j, 0)),
                  pl.BlockSpec((1, POOL_HALO, d), lambda i, j: (i, 0, 0)),
                  pl.BlockSpec((1, d), lambda i, j: (0, 0)),
                  _resident((ng, gd, gd)),
                  pl.BlockSpec((1, d), lambda i, j: (0, 0))],
        out_specs=(pl.BlockSpec((1, tt, d), lambda i, j: (i, j, 0)),
                   pl.BlockSpec((1, POOL_HALO, d), lambda i, j: (i, 0, 0))),
        scratch_shapes=[pltpu.VMEM((tt + POOL_HALO, d), f32)],
        compiler_params=_params(("parallel", "arbitrary"), vmem),
        name="pool_layer",
    )(x, prev16, g.reshape(1, d), w_pool, scale.reshape(1, d))


def _s5_kernel(x_ref, g_ref, h0r_ref, h0i_ref, are_ref, aim_ref, bblk_ref, cblk_ref, d_ref,
               gout_ref, sr_ref, si_ref, u_ref, bu_ref, gel_ref, *, nb, tt):
    t = pl.program_id(0)
    npack = u_ref.shape[0]
    nsl = (S5_PACK * S5_STATE) // LANES

    @pl.when(t == 0)
    def _():
        sr_ref[...] = h0r_ref[...]
        si_ref[...] = h0i_ref[...]

    g = g_ref[...]
    for b in range(nb):
        u = _rms(x_ref[b], g)
        for p in range(npack):
            u_ref[p, b * tt:(b + 1) * tt, :] = u[:, p * LANES:(p + 1) * LANES]

    def pack_body(p, carry):
        u = u_ref[p]
        bu = jnp.dot(u.astype(bf16), bblk_ref[p], preferred_element_type=f32)
        for j in range(2 * nsl):
            bu_ref[j] = bu[:, j * LANES:(j + 1) * LANES]
        a_re = are_ref[p]
        a_im = aim_ref[p]
        ar = [jnp.broadcast_to(a_re[:, j * LANES:(j + 1) * LANES], (nb, LANES)) for j in range(nsl)]
        ai = [jnp.broadcast_to(a_im[:, j * LANES:(j + 1) * LANES], (nb, LANES)) for j in range(nsl)]
        s0r = sr_ref[p]
        s0i = si_ref[p]
        init = (tuple(s0r[:, j * LANES:(j + 1) * LANES] for j in range(nsl)),
                tuple(s0i[:, j * LANES:(j + 1) * LANES] for j in range(nsl)))

        def step(k, s):
            s_re, s_im = s
            new_re, new_im = [], []
            for j in range(nsl):
                rows = pl.ds(k, nb, stride=tt)
                b_re = bu_ref.at[j][rows, :]
                b_im = bu_ref.at[nsl + j][rows, :]
                n_re = ar[j] * s_re[j] - ai[j] * s_im[j] + b_re
                n_im = ar[j] * s_im[j] + ai[j] * s_re[j] + b_im
                bu_ref.at[j][rows, :] = n_re
                bu_ref.at[nsl + j][rows, :] = n_im
                new_re.append(n_re)
                new_im.append(n_im)
            return tuple(new_re), tuple(new_im)

        s_re, s_im = lax.fori_loop(0, tt, step, init)
        sr_ref[p] = jnp.concatenate(s_re, axis=-1)
        si_ref[p] = jnp.concatenate(s_im, axis=-1)
        states = jnp.concatenate([bu_ref[j] for j in range(2 * nsl)], axis=-1).astype(bf16)
        y = jnp.dot(states, cblk_ref[p], preferred_element_type=f32) + d_ref[p] * u
        gel_ref[p] = (0.5 * y * (1.0 + lax.erf(y * math.sqrt(0.5)))).astype(bf16)
        return carry

    lax.fori_loop(0, npack, pack_body, 0)
    for b in range(nb):
        for p in range(npack):
            gout_ref[b, :, p * LANES:(p + 1) * LANES] = gel_ref[p, b * tt:(b + 1) * tt, :]


def _s5_scan(x, g, h0_re, h0_im, a_re, a_im, bblk, cblk, dvec, *, tt):
    b, t, d = x.shape
    npack = d // LANES
    sw = S5_PACK * S5_STATE
    rows = b * tt
    vmem = (2 * rows * d * 4 + rows * d * 4 + rows * 2 * sw * 4 + rows * d * 2 + 2 * rows * d * 2
            + 2 * npack * LANES * 2 * sw * 2 + 3 * rows * 2 * sw * 4)
    st_spec = pl.BlockSpec((npack, b, sw), lambda i: (0, 0, 0))
    return pl.pallas_call(
        functools.partial(_s5_kernel, nb=b, tt=tt),
        out_shape=(jax.ShapeDtypeStruct((b, t, d), bf16),
                   jax.ShapeDtypeStruct((npack, b, sw), f32),
                   jax.ShapeDtypeStruct((npack, b, sw), f32)),
        grid=(t // tt,),
        in_specs=[pl.BlockSpec((b, tt, d), lambda i: (0, i, 0)),
                  pl.BlockSpec((1, d), lambda i: (0, 0)),
                  st_spec, st_spec,
                  _resident((npack, 1, sw)), _resident((npack, 1, sw)),
                  _resident((npack, LANES, 2 * sw)), _resident((npack, 2 * sw, LANES)),
                  _resident((npack, 1, LANES))],
        out_specs=(pl.BlockSpec((b, tt, d), lambda i: (0, i, 0)), st_spec, st_spec),
        scratch_shapes=[pltpu.VMEM((npack, rows, LANES), f32),
                        pltpu.VMEM((2 * sw // LANES, rows, LANES), f32),
                        pltpu.VMEM((npack, rows, LANES), bf16)],
        compiler_params=_params(("arbitrary",), vmem),
        name="s5_scan",
    )(x, g.reshape(1, d), h0_re, h0_im, a_re, a_im, bblk, cblk, dvec)


def _s5_discretise(a_re, a_im, b_re, b_im, c_re, c_im, log_dt):
    g, n = a_re.shape
    c = b_re.shape[-1]
    npack = g // S5_PACK
    dt = jnp.exp(log_dt)[:, None]
    mag = jnp.exp(dt * a_re)
    abar_re = mag * jnp.cos(dt * a_im)
    abar_im = mag * jnp.sin(dt * a_im)
    den = a_re * a_re + a_im * a_im
    nr = abar_re - 1.0
    f_re = (nr * a_re + abar_im * a_im) / den
    f_im = (abar_im * a_re - nr * a_im) / den
    bbar_re = f_re[..., None] * b_re - f_im[..., None] * b_im
    bbar_im = f_re[..., None] * b_im + f_im[..., None] * b_re
    eye = jnp.eye(S5_PACK, dtype=f32)

    def in_blocks(m):
        m = jnp.swapaxes(m, 1, 2).reshape(npack, S5_PACK, c, n)
        return jnp.einsum('pgcn,gh->pgchn', m, eye).reshape(npack, S5_PACK * c, S5_PACK * n)

    def out_blocks(m):
        m = jnp.swapaxes(m, 1, 2).reshape(npack, S5_PACK, n, c)
        return jnp.einsum('pgnc,gh->pgnhc', m, eye).reshape(npack, S5_PACK * n, S5_PACK * c)

    bblk = jnp.concatenate([in_blocks(bbar_re), in_blocks(bbar_im)], axis=2).astype(bf16)
    cblk = jnp.concatenate([out_blocks(c_re), -out_blocks(c_im)], axis=1).astype(bf16)
    pack = lambda v: v.reshape(npack, 1, S5_PACK * n)
    return pack(abar_re), pack(abar_im), bblk, cblk


def _pack_state(s):
    b, g, n = s.shape
    return jnp.swapaxes(s.reshape(b, g // S5_PACK, S5_PACK * n), 0, 1)


def _unpack_state(s):
    npack, b, sw = s.shape
    return jnp.swapaxes(s, 0, 1).reshape(b, npack * S5_PACK, S5_STATE)


def _qkv_kernel(x_ref, g_ref, w_ref, gq_ref, gk_ref, q_ref, k_ref, v_ref, h_ref, *, nblk):
    n = pl.program_id(1)

    @pl.when(n == 0)
    def _():
        h_ref[...] = _rms(x_ref[...], g_ref[...]).astype(bf16)

    r = jnp.dot(h_ref[...], w_ref[...], preferred_element_type=f32)

    def head_norm(gain):
        heads = [r[:, i * SB_HEAD_DIM:(i + 1) * SB_HEAD_DIM] for i in range(r.shape[1] // SB_HEAD_DIM)]
        return jnp.concatenate([_rms(c, gain) for c in heads], axis=-1)

    @pl.when(n < nblk)
    def _():
        q_ref[...] = head_norm(gq_ref[...]).astype(bf16)

    @pl.when((n >= nblk) & (n < 2 * nblk))
    def _():
        k_ref[...] = head_norm(gk_ref[...])

    @pl.when(n >= 2 * nblk)
    def _():
        v_ref[...] = r


def _qkv_proj(x, g, w_qkv, g_q, g_k, *, tm, tn=512):
    m, d = x.shape
    nblk = d // tn
    vmem = 2 * tm * d * 4 + tm * d * 2 + 2 * d * tn * 2 + 2 * 3 * tm * tn * 4 + 3 * tm * tn * 4

    def out_spec(which):
        return pl.BlockSpec((tm, tn), lambda i, n: (i, jnp.clip(n - which * nblk, 0, nblk - 1)))

    return pl.pallas_call(
        functools.partial(_qkv_kernel, nblk=nblk),
        out_shape=(jax.ShapeDtypeStruct((m, d), bf16), jax.ShapeDtypeStruct((m, d), f32),
                   jax.ShapeDtypeStruct((m, d), f32)),
        grid=(m // tm, 3 * nblk),
        in_specs=[pl.BlockSpec((tm, d), lambda i, n: (i, 0)),
                  pl.BlockSpec((1, d), lambda i, n: (0, 0)),
                  pl.BlockSpec((d, tn), lambda i, n: (0, n)),
                  pl.BlockSpec((1, SB_HEAD_DIM), lambda i, n: (0, 0)),
                  pl.BlockSpec((1, SB_HEAD_DIM), lambda i, n: (0, 0))],
        out_specs=(out_spec(0), out_spec(1), out_spec(2)),
        scratch_shapes=[pltpu.VMEM((tm, d), bf16)],
        compiler_params=_params(("parallel", "arbitrary"), vmem),
        name="qkv_proj",
    )(x, g.reshape(1, d), w_qkv, g_q.reshape(1, SB_HEAD_DIM), g_k.reshape(1, SB_HEAD_DIM))


def _later_matrix(tk):
    j = lax.broadcasted_iota(jnp.int32, (tk, tk), 0)
    s = lax.broadcasted_iota(jnp.int32, (tk, tk), 1)
    return jnp.where(j > s, 1.0, 0.0).astype(bf16)


def _sb_block(z, vb, carry, later_m, mask):
    sp = _softplus(z)
    if mask is not None:
        sp = jnp.where(mask, sp, 0.0)
    hi = sp.astype(bf16)
    lo = (sp - hi.astype(f32)).astype(bf16)
    later = (jnp.dot(hi, later_m, preferred_element_type=f32)
             + jnp.dot(lo, later_m, preferred_element_type=f32))
    w = jnp.exp(z - sp - later - carry)
    if mask is not None:
        w = jnp.where(mask, w, 0.0)
    out = jnp.dot(w.astype(bf16), vb, preferred_element_type=f32)
    return out, carry + jnp.sum(sp, axis=-1, keepdims=True)


def _sb_prompt_kernel(bias_ref, q_ref, k_ref, v_ref, o_ref, kb_ref, vb_ref, *, tq):
    h = pl.program_id(1)
    qi = pl.program_id(2)

    @pl.when(qi == 0)
    def _():
        kb_ref[...] = k_ref[0].astype(bf16)
        vb_ref[...] = v_ref[0].astype(bf16)

    q = q_ref[0]
    bias = bias_ref[h]
    scale = 1.0 / math.sqrt(SB_HEAD_DIM)
    later_m = _later_matrix(tq)

    def logits(rows):
        return lax.dot_general(q, kb_ref[rows, :], _NT, preferred_element_type=f32) * scale + bias

    r = lax.broadcasted_iota(jnp.int32, (tq, tq), 0)
    c = lax.broadcasted_iota(jnp.int32, (tq, tq), 1)
    diag = pl.ds(pl.multiple_of(qi * tq, tq), tq)
    acc, carry = _sb_block(logits(diag), vb_ref[diag, :], jnp.zeros((tq, 1), f32), later_m, c < r)

    def body(i, state):
        acc, carry = state
        rows = pl.ds(pl.multiple_of((qi - 1 - i) * tq, tq), tq)
        out, carry = _sb_block(logits(rows), vb_ref[rows, :], carry, later_m, None)
        return acc + out, carry

    acc, _ = lax.fori_loop(0, qi, body, (acc, carry))
    o_ref[0] = acc.astype(bf16)


def _sb_prompt(q, k, v, bias, *, tq):
    b, t, d = q.shape
    nh = d // SB_HEAD_DIM
    vmem = 2 * 2 * t * SB_HEAD_DIM * 4 + 2 * t * SB_HEAD_DIM * 2 + 8 * tq * tq * 4
    return pl.pallas_call(
        functools.partial(_sb_prompt_kernel, tq=tq),
        out_shape=jax.ShapeDtypeStruct((b, t, d), bf16),
        grid=(b, nh, t // tq),
        in_specs=[pl.BlockSpec(memory_space=pltpu.SMEM),
                  pl.BlockSpec((1, tq, SB_HEAD_DIM), lambda i, h, j: (i, j, h)),
                  pl.BlockSpec((1, t, SB_HEAD_DIM), lambda i, h, j: (i, 0, h)),
                  pl.BlockSpec((1, t, SB_HEAD_DIM), lambda i, h, j: (i, 0, h))],
        out_specs=pl.BlockSpec((1, tq, SB_HEAD_DIM), lambda i, h, j: (i, j, h)),
        scratch_shapes=[pltpu.VMEM((t, SB_HEAD_DIM), bf16), pltpu.VMEM((t, SB_HEAD_DIM), bf16)],
        compiler_params=_params(("parallel", "parallel", "arbitrary"), vmem),
        name="sb_prompt",
    )(bias, q, k, v)


def _sb_decode_kernel(pt_ref, q_ref, kn_ref, vn_ref, kp_ref, vp_ref, bias_ref, o_ref,
                      qbd_ref, qstage_ref, acc_ref, carry_ref, *, ts, nh):
    i = pl.program_id(1)
    nrow = nh * ts
    dh = SB_HEAD_DIM
    scale = 1.0 / math.sqrt(dh)

    @pl.when(i == 0)
    def _():
        qstage_ref[...] = jnp.zeros_like(qstage_ref)
        for h in range(nh):
            qstage_ref[h * ts:(h + 1) * ts, h * dh:(h + 1) * dh] = q_ref[0, :, h * dh:(h + 1) * dh]
        qbd_ref[...] = qstage_ref[...].astype(bf16)
        acc_ref[...] = jnp.zeros_like(acc_ref)
        carry_ref[...] = jnp.zeros_like(carry_ref)

    def block(kp, vp, mask):
        z = lax.dot_general(qbd_ref[...], kp.astype(bf16), _NT, preferred_element_type=f32)
        z = z * scale + bias_ref[...]
        out, carry = _sb_block(z, vp.astype(bf16), carry_ref[...], _later_matrix(kp.shape[0]), mask)
        carry_ref[...] = carry
        for h in range(nh):
            acc_ref[h * ts:(h + 1) * ts, :] += out[h * ts:(h + 1) * ts, h * dh:(h + 1) * dh]

    @pl.when(i == 0)
    def _():
        nk = kn_ref.shape[1]
        t_of_row = lax.broadcasted_iota(jnp.int32, (nrow, nk), 0) % ts
        j = lax.broadcasted_iota(jnp.int32, (nrow, nk), 1)
        block(kn_ref[0], vn_ref[0], j < t_of_row)

    @pl.when(i > 0)
    def _():
        block(kp_ref[0], vp_ref[0], None)

    @pl.when(i == pl.num_programs(1) - 1)
    def _():
        for h in range(nh):
            o_ref[0, :, h * dh:(h + 1) * dh] = acc_ref[h * ts:(h + 1) * ts, :]


def _sb_decode(q, k_new, v_new, cache_k, cache_v, page_table, bias_rows):
    b, ts, d = q.shape
    nh = d // SB_HEAD_DIM
    n_pages = page_table.shape[1]
    nrow = nh * ts
    page_map = lambda i, j, pt: (pt[i, n_pages - jnp.maximum(j, 1)], 0, 0)
    per_b = lambda i, j, pt: (i, 0, 0)
    vmem = 2 * 2 * 2 * PAGE_SIZE * d * 4 + nrow * d * 2 + 6 * nrow * d * 4
    return pl.pallas_call(
        functools.partial(_sb_decode_kernel, ts=ts, nh=nh),
        out_shape=jax.ShapeDtypeStruct((b, ts, d), f32),
        grid_spec=pltpu.PrefetchScalarGridSpec(
            num_scalar_prefetch=1,
            grid=(b, n_pages + 1),
            in_specs=[pl.BlockSpec((1, ts, d), per_b),
                      pl.BlockSpec((1, PAGE_SIZE, d), per_b),
                      pl.BlockSpec((1, PAGE_SIZE, d), per_b),
                      pl.BlockSpec((1, PAGE_SIZE, d), page_map),
                      pl.BlockSpec((1, PAGE_SIZE, d), page_map),
                      pl.BlockSpec((nrow, 1), lambda i, j, pt: (0, 0))],
            out_specs=pl.BlockSpec((1, ts, d), per_b),
            scratch_shapes=[pltpu.VMEM((nrow, d), bf16), pltpu.VMEM((nrow, d), f32),
                            pltpu.VMEM((nrow, SB_HEAD_DIM), f32),
                            pltpu.VMEM((nrow, 1), f32)]),
        compiler_params=_params(("parallel", "arbitrary"), vmem),
        name="sb_decode",
    )(page_table, q, k_new, v_new, cache_k, cache_v, bias_rows)


def kernel(x_prompt, x_sample, cache_pool, state_ssm_re, state_ssm_im, cache_k, cache_v, page_table,
           norm_mix, norm_ffn, w_ffn_gate, w_ffn_up, w_ffn_down, w_pool, pool_scale,
           ssm_a_re, ssm_a_im, ssm_b_re, ssm_b_im, ssm_c_re, ssm_c_im, ssm_d, ssm_log_dt,
           w_glu_a, w_glu_b, w_qkv, w_o, sb_q_norm, sb_k_norm, sb_bias):
    bp, tp, d = x_prompt.shape
    bs, ts, _ = x_sample.shape
    depth = norm_mix.shape[0]
    nh = d // SB_HEAD_DIM
    mp, ms = bp * tp, bs * ts
    past_len = page_table.shape[1] * PAGE_SIZE
    n_phys = cache_k.shape[1]
    tm = TM

    xp, xs = x_prompt, x_sample
    pool_p, pool_s = [], []
    ssm_p, ssm_s = [], []
    kv_p, kv_s = [], []
    for i in range(depth):
        kind = i % N_MIXERS
        j = i // N_MIXERS
        if kind == 0:
            w = w_pool[j].astype(bf16)
            zero_prev = jnp.zeros((bp, POOL_HALO, d), f32)
            prev = jnp.pad(cache_pool[j], ((0, 0), (1, 0), (0, 0)))
            xp, buf_p = _pool_layer(xp, zero_prev, norm_mix[i], w, pool_scale[j], tt=POOL_TT, start_pos=0)
            xs, buf_s = _pool_layer(xs, prev, norm_mix[i], w, pool_scale[j], tt=ts, start_pos=past_len)
            pool_p.append(buf_p[:, 1:])
            pool_s.append(buf_s[:, 1:])
        elif kind == 1:
            a_re, a_im, bblk, cblk = _s5_discretise(ssm_a_re[j], ssm_a_im[j], ssm_b_re[j], ssm_b_im[j],
                                                    ssm_c_re[j], ssm_c_im[j], ssm_log_dt[j])
            dvec = ssm_d[j].reshape(d // LANES, 1, LANES)
            zero_state = jnp.zeros((d // LANES, bp, S5_PACK * S5_STATE), f32)
            gp, sr_p, si_p = _s5_scan(xp, norm_mix[i], zero_state, zero_state, a_re, a_im, bblk, cblk, dvec,
                                      tt=S5_TT)
            gs, sr_s, si_s = _s5_scan(xs, norm_mix[i], _pack_state(state_ssm_re[j]), _pack_state(state_ssm_im[j]),
                                      a_re, a_im, bblk, cblk, dvec, tt=ts)
            wa, wb = w_glu_a[j].astype(bf16), w_glu_b[j].astype(bf16)
            xp = _glu_residual(xp.reshape(mp, d), gp.reshape(mp, d), wa, wb, tm=TM_GLU).reshape(bp, tp, d)
            xs = _glu_residual(xs.reshape(ms, d), gs.reshape(ms, d), wa, wb, tm=ms).reshape(bs, ts, d)
            ssm_p.append((_unpack_state(sr_p), _unpack_state(si_p)))
            ssm_s.append((_unpack_state(sr_s), _unpack_state(si_s)))
        else:
            wqkv, wo = w_qkv[j].astype(bf16), w_o[j].astype(bf16)
            q_p, k_p, v_p = _qkv_proj(xp.reshape(mp, d), norm_mix[i], wqkv, sb_q_norm[j], sb_k_norm[j], tm=tm)
            q_s, k_s, v_s = _qkv_proj(xs.reshape(ms, d), norm_mix[i], wqkv, sb_q_norm[j], sb_k_norm[j], tm=ms)
            o_p = _sb_prompt(q_p.reshape(bp, tp, d), k_p.reshape(bp, tp, d), v_p.reshape(bp, tp, d), sb_bias[j],
                             tq=TQ)
            pad_new = lambda a: jnp.pad(a.reshape(bs, ts, d), ((0, 0), (0, PAGE_SIZE - ts), (0, 0)))
            o_s = _sb_decode(q_s.reshape(bs, ts, d).astype(f32), pad_new(k_s), pad_new(v_s),
                             cache_k.reshape(-1, PAGE_SIZE, d), cache_v.reshape(-1, PAGE_SIZE, d),
                             page_table + j * n_phys, jnp.repeat(sb_bias[j], ts).reshape(nh * ts, 1))
            xp = _proj_residual(xp.reshape(mp, d), o_p.reshape(mp, d), wo, tm=tm).reshape(bp, tp, d)
            xs = _proj_residual(xs.reshape(ms, d), o_s.reshape(ms, d).astype(bf16), wo, tm=ms).reshape(bs, ts, d)
            kv_p.append((k_p.reshape(bp, tp, nh, SB_HEAD_DIM), v_p.reshape(bp, tp, nh, SB_HEAD_DIM)))
            kv_s.append((k_s.reshape(bs, ts, nh, SB_HEAD_DIM), v_s.reshape(bs, ts, nh, SB_HEAD_DIM)))
        wg, wu, wd = w_ffn_gate[i].astype(bf16), w_ffn_up[i].astype(bf16), w_ffn_down[i].astype(bf16)
        xp = _ffn(xp.reshape(mp, d), norm_ffn[i], wg, wu, wd, tm=tm).reshape(bp, tp, d)
        xs = _ffn(xs.reshape(ms, d), norm_ffn[i], wg, wu, wd, tm=ms).reshape(bs, ts, d)

    stack = lambda items: jnp.stack(items, axis=0)
    return (xp, xs, stack(pool_p), stack(pool_s),
            stack([s[0] for s in ssm_p]), stack([s[1] for s in ssm_p]),
            stack([s[0] for s in ssm_s]), stack([s[1] for s in ssm_s]),
            stack([kv[0] for kv in kv_p]), stack([kv[1] for kv in kv_p]),
            stack([kv[0] for kv in kv_s]), stack([kv[1] for kv in kv_s]))
```

```python
import functools
import math

import jax
import jax.numpy as jnp
from jax import lax
from jax.experimental import pallas as pl
from jax.experimental.pallas import tpu as pltpu

f32 = jnp.float32
bf16 = jnp.bfloat16

RMS_EPS = 1e-6
N_MIXERS = 3
POOL_WINDOWS = (2, 4, 8, 16)
POOL_HALO = 16
S5_GROUP_SIZE = 16
S5_STATE = 64
S5_PACK = 8
SB_HEAD_DIM = 128
PAGE_SIZE = 128
LANES = 128
MIB = 1024 * 1024
VMEM_CAP = 56 * MIB

TM = 512
TM_GLU = 256
POOL_TT = 512
S5_TT = 128
TQ = 256
SCAN_UNROLL = 8
DECODE_PAGES = 4

_NT = (((1,), (1,)), ((), ()))


def _params(sem, vmem_bytes):
    return pltpu.CompilerParams(dimension_semantics=sem,
                                vmem_limit_bytes=int(min(VMEM_CAP, vmem_bytes + 12 * MIB)))


def _rms(x, g):
    return x * lax.rsqrt(jnp.mean(x * x, axis=-1, keepdims=True) + RMS_EPS) * g


def _resident(shape):
    nd = len(shape)
    return pl.BlockSpec(shape, lambda *_: (0,) * nd, pipeline_mode=pl.Buffered(1))


def _ffn_kernel(x_ref, g_ref, wg_ref, wu_ref, wd_ref, o_ref, h_ref):
    f = pl.program_id(1)

    @pl.when(f == 0)
    def _():
        x = x_ref[...]
        h_ref[...] = _rms(x, g_ref[...]).astype(bf16)
        o_ref[...] = x

    h = h_ref[...]
    a = jnp.dot(h, wg_ref[...], preferred_element_type=f32)
    b = jnp.dot(h, wu_ref[...], preferred_element_type=f32)
    act = (a * jax.nn.sigmoid(a) * b).astype(bf16)
    o_ref[...] += jnp.dot(act, wd_ref[...], preferred_element_type=f32)


def _ffn(x, g, wg, wu, wd, *, tm, tf=512):
    m, d = x.shape
    nf = wg.shape[1] // tf
    vmem = 2 * 2 * tm * d * 4 + tm * d * 2 + 2 * 3 * d * tf * 2 + 3 * tm * tf * 4
    return pl.pallas_call(
        _ffn_kernel,
        out_shape=jax.ShapeDtypeStruct((m, d), f32),
        grid=(m // tm, nf),
        in_specs=[pl.BlockSpec((tm, d), lambda i, j: (i, 0)),
                  pl.BlockSpec((1, d), lambda i, j: (0, 0)),
                  pl.BlockSpec((d, tf), lambda i, j: (0, j)),
                  pl.BlockSpec((d, tf), lambda i, j: (0, j)),
                  pl.BlockSpec((tf, d), lambda i, j: (j, 0))],
        out_specs=pl.BlockSpec((tm, d), lambda i, j: (i, 0)),
        scratch_shapes=[pltpu.VMEM((tm, d), bf16)],
        compiler_params=_params(("parallel", "arbitrary"), vmem),
        name="ffn",
    )(x, g.reshape(1, d), wg, wu, wd)


def _proj_kernel(x_ref, a_ref, w_ref, o_ref):
    o_ref[...] = x_ref[...] + jnp.dot(a_ref[...], w_ref[...], preferred_element_type=f32)


def _proj_residual(x, a, w, *, tm):
    m, d = x.shape
    k = a.shape[1]
    vmem = 2 * 2 * tm * d * 4 + 2 * tm * k * 2 + k * d * 2 + tm * d * 4
    return pl.pallas_call(
        _proj_kernel,
        out_shape=jax.ShapeDtypeStruct((m, d), f32),
        grid=(m // tm,),
        in_specs=[pl.BlockSpec((tm, d), lambda i: (i, 0)),
                  pl.BlockSpec((tm, k), lambda i: (i, 0)),
                  _resident((k, d))],
        out_specs=pl.BlockSpec((tm, d), lambda i: (i, 0)),
        compiler_params=_params(("parallel",), vmem),
        name="proj_residual",
    )(x, a, w)


def _glu_kernel(x_ref, a_ref, wa_ref, wb_ref, o_ref):
    a = a_ref[...]
    p = jnp.dot(a, wa_ref[...], preferred_element_type=f32)
    q = jnp.dot(a, wb_ref[...], preferred_element_type=f32)
    o_ref[...] = x_ref[...] + p * jax.nn.sigmoid(q)


def _glu_residual(x, a, wa, wb, *, tm):
    m, d = x.shape
    k = a.shape[1]
    vmem = 2 * 2 * tm * d * 4 + 2 * tm * k * 2 + 2 * k * d * 2 + 3 * tm * d * 4
    return pl.pallas_call(
        _glu_kernel,
        out_shape=jax.ShapeDtypeStruct((m, d), f32),
        grid=(m // tm,),
        in_specs=[pl.BlockSpec((tm, d), lambda i: (i, 0)),
                  pl.BlockSpec((tm, k), lambda i: (i, 0)),
                  _resident((k, d)), _resident((k, d))],
        out_specs=pl.BlockSpec((tm, d), lambda i: (i, 0)),
        compiler_params=_params(("parallel",), vmem),
        name="glu_residual",
    )(x, a, wa, wb)


def _pool_kernel(x_ref, prev_ref, g_ref, w_ref, sc_ref, o_ref, buf_ref, ext_ref, *, tt, start_pos):
    t = pl.program_id(1)
    halo = POOL_HALO

    @pl.when(t == 0)
    def _():
        ext_ref[0:halo, :] = prev_ref[0]

    @pl.when(t > 0)
    def _():
        ext_ref[0:halo, :] = ext_ref[tt:tt + halo, :]

    x = x_ref[0]
    h = _rms(x, g_ref[...])
    ext_ref[halo:halo + tt, :] = h
    buf_ref[0] = ext_ref[tt:tt + halo, :]

    gd = x.shape[1] // len(POOL_WINDOWS)
    pos = start_pos + t * tt + lax.broadcasted_iota(jnp.int32, (tt, 1), 0)
    for gi, w in enumerate(POOL_WINDOWS):
        sl = slice(gi * gd, (gi + 1) * gd)
        win = h[:, sl]
        for i in range(1, w):
            win = win + ext_ref[halo - i:halo - i + tt, sl]
        cnt = jnp.minimum(pos + 1, w).astype(f32)
        p = win / cnt - h[:, sl]
        y = jnp.dot(p.astype(bf16), w_ref[gi], preferred_element_type=f32)
        o_ref[0, :, sl] = x[:, sl] + y * sc_ref[:, sl]


def _pool_layer(x, prev16, g, w_pool, scale, *, tt, start_pos):
    b, t, d = x.shape
    ng, gd = w_pool.shape[0], w_pool.shape[1]
    vmem = 2 * 2 * tt * d * 4 + (tt + POOL_HALO) * d * 4 + ng * gd * gd * 2 + 4 * tt * d * 4
    return pl.pallas_call(
        functools.partial(_pool_kernel, tt=tt, start_pos=start_pos),
        out_shape=(jax.ShapeDtypeStruct((b, t, d), f32), jax.ShapeDtypeStruct((b, POOL_HALO, d), f32)),
        grid=(b, t // tt),
        in_specs=[pl.BlockSpec((1, tt, d), lambda i, j: (i, j, 0)),
                  pl.BlockSpec((1, POOL_HALO, d), lambda i, j: (i, 0, 0)),
                  pl.BlockSpec((1, d), lambda i, j: (0, 0)),
                  _resident((ng, gd, gd)),
                  pl.BlockSpec((1, d), lambda i, j: (0, 0))],
        out_specs=(pl.BlockSpec((1, tt, d), lambda i, j: (i, j, 0)),
                   pl.BlockSpec((1, POOL_HALO, d), lambda i, j: (i, 0, 0))),
        scratch_shapes=[pltpu.VMEM((tt + POOL_HALO, d), f32)],
        compiler_params=_params(("parallel", "arbitrary"), vmem),
        name="pool_layer",
    )(x, prev16, g.reshape(1, d), w_pool, scale.reshape(1, d))


def _s5_kernel(x_ref, g_ref, h0r_ref, h0i_ref, are_ref, aim_ref, bblk_ref, cblk_ref, d_ref,
               gout_ref, sr_ref, si_ref, u_ref, bu_ref, gel_ref, *, nb, tt):
    t = pl.program_id(0)
    npack = u_ref.shape[0]
    nsl = (S5_PACK * S5_STATE) // LANES

    @pl.when(t == 0)
    def _():
        sr_ref[...] = h0r_ref[...]
        si_ref[...] = h0i_ref[...]

    g = g_ref[...]
    for b in range(nb):
        u = _rms(x_ref[b], g)
        for p in range(npack):
            u_ref.at[p][pl.ds(b, tt, stride=nb), :] = u[:, p * LANES:(p + 1) * LANES]

    def pack_body(p, carry):
        u = u_ref[p]
        bu = jnp.dot(u.astype(bf16), bblk_ref[p], preferred_element_type=f32)
        for j in range(2 * nsl):
            bu_ref[j] = bu[:, j * LANES:(j + 1) * LANES]
        a_re = are_ref[p]
        a_im = aim_ref[p]
        ar = [jnp.broadcast_to(a_re[:, j * LANES:(j + 1) * LANES], (nb, LANES)) for j in range(nsl)]
        ai = [jnp.broadcast_to(a_im[:, j * LANES:(j + 1) * LANES], (nb, LANES)) for j in range(nsl)]
        s0r = sr_ref[p]
        s0i = si_ref[p]
        init = (tuple(s0r[:, j * LANES:(j + 1) * LANES] for j in range(nsl)),
                tuple(s0i[:, j * LANES:(j + 1) * LANES] for j in range(nsl)))

        def step(k, s):
            s_re, s_im = s
            new_re, new_im = [], []
            for j in range(nsl):
                rows = pl.ds(pl.multiple_of(k * nb, nb), nb)
                b_re = bu_ref.at[j][rows, :]
                b_im = bu_ref.at[nsl + j][rows, :]
                n_re = ar[j] * s_re[j] - ai[j] * s_im[j] + b_re
                n_im = ar[j] * s_im[j] + ai[j] * s_re[j] + b_im
                bu_ref.at[j][rows, :] = n_re
                bu_ref.at[nsl + j][rows, :] = n_im
                new_re.append(n_re)
                new_im.append(n_im)
            return tuple(new_re), tuple(new_im)

        s_re, s_im = lax.fori_loop(0, tt, step, init, unroll=SCAN_UNROLL)
        sr_ref[p] = jnp.concatenate(s_re, axis=-1)
        si_ref[p] = jnp.concatenate(s_im, axis=-1)
        states = jnp.concatenate([bu_ref[j] for j in range(2 * nsl)], axis=-1).astype(bf16)
        y = jnp.dot(states, cblk_ref[p], preferred_element_type=f32) + d_ref[p] * u
        gel_ref[p] = 0.5 * y * (1.0 + lax.erf(y * math.sqrt(0.5)))
        return carry

    lax.fori_loop(0, npack, pack_body, 0)
    for b in range(nb):
        for p in range(npack):
            gout_ref[b, :, p * LANES:(p + 1) * LANES] = gel_ref.at[p][pl.ds(b, tt, stride=nb), :].astype(bf16)


def _s5_scan(x, g, h0_re, h0_im, a_re, a_im, bblk, cblk, dvec, *, tt):
    b, t, d = x.shape
    npack = d // LANES
    sw = S5_PACK * S5_STATE
    rows = b * tt
    vmem = (2 * rows * d * 4 + rows * d * 4 + rows * 2 * sw * 4 + rows * d * 4 + 2 * rows * d * 2
            + 2 * npack * LANES * 2 * sw * 2 + 3 * rows * 2 * sw * 4)
    st_spec = pl.BlockSpec((npack, b, sw), lambda i: (0, 0, 0))
    return pl.pallas_call(
        functools.partial(_s5_kernel, nb=b, tt=tt),
        out_shape=(jax.ShapeDtypeStruct((b, t, d), bf16),
                   jax.ShapeDtypeStruct((npack, b, sw), f32),
                   jax.ShapeDtypeStruct((npack, b, sw), f32)),
        grid=(t // tt,),
        in_specs=[pl.BlockSpec((b, tt, d), lambda i: (0, i, 0)),
                  pl.BlockSpec((1, d), lambda i: (0, 0)),
                  st_spec, st_spec,
                  _resident((npack, 1, sw)), _resident((npack, 1, sw)),
                  _resident((npack, LANES, 2 * sw)), _resident((npack, 2 * sw, LANES)),
                  _resident((npack, 1, LANES))],
        out_specs=(pl.BlockSpec((b, tt, d), lambda i: (0, i, 0)), st_spec, st_spec),
        scratch_shapes=[pltpu.VMEM((npack, rows, LANES), f32),
                        pltpu.VMEM((2 * sw // LANES, rows, LANES), f32),
                        pltpu.VMEM((npack, rows, LANES), f32)],
        compiler_params=_params(("arbitrary",), vmem),
        name="s5_scan",
    )(x, g.reshape(1, d), h0_re, h0_im, a_re, a_im, bblk, cblk, dvec)


def _s5_discretise(a_re, a_im, b_re, b_im, c_re, c_im, log_dt):
    g, n = a_re.shape
    c = b_re.shape[-1]
    npack = g // S5_PACK
    dt = jnp.exp(log_dt)[:, None]
    mag = jnp.exp(dt * a_re)
    abar_re = mag * jnp.cos(dt * a_im)
    abar_im = mag * jnp.sin(dt * a_im)
    den = a_re * a_re + a_im * a_im
    nr = abar_re - 1.0
    f_re = (nr * a_re + abar_im * a_im) / den
    f_im = (abar_im * a_re - nr * a_im) / den
    bbar_re = f_re[..., None] * b_re - f_im[..., None] * b_im
    bbar_im = f_re[..., None] * b_im + f_im[..., None] * b_re
    eye = jnp.eye(S5_PACK, dtype=f32)

    def in_blocks(m):
        m = jnp.swapaxes(m, 1, 2).reshape(npack, S5_PACK, c, n)
        return jnp.einsum('pgcn,gh->pgchn', m, eye).reshape(npack, S5_PACK * c, S5_PACK * n)

    def out_blocks(m):
        m = jnp.swapaxes(m, 1, 2).reshape(npack, S5_PACK, n, c)
        return jnp.einsum('pgnc,gh->pgnhc', m, eye).reshape(npack, S5_PACK * n, S5_PACK * c)

    bblk = jnp.concatenate([in_blocks(bbar_re), in_blocks(bbar_im)], axis=2).astype(bf16)
    cblk = jnp.concatenate([out_blocks(c_re), -out_blocks(c_im)], axis=1).astype(bf16)
    pack = lambda v: v.reshape(npack, 1, S5_PACK * n)
    return pack(abar_re), pack(abar_im), bblk, cblk


def _pack_state(s):
    b, g, n = s.shape
    return jnp.swapaxes(s.reshape(b, g // S5_PACK, S5_PACK * n), 0, 1)


def _unpack_state(s):
    npack, b, sw = s.shape
    return jnp.swapaxes(s, 0, 1).reshape(b, npack * S5_PACK, S5_STATE)


def _qkv_kernel(x_ref, g_ref, w_ref, gq_ref, gk_ref, q_ref, k_ref, v_ref, h_ref, *, nblk):
    n = pl.program_id(1)

    @pl.when(n == 0)
    def _():
        h_ref[...] = _rms(x_ref[...], g_ref[...]).astype(bf16)

    r = jnp.dot(h_ref[...], w_ref[...], preferred_element_type=f32)

    def head_norm(gain):
        heads = [r[:, i * SB_HEAD_DIM:(i + 1) * SB_HEAD_DIM] for i in range(r.shape[1] // SB_HEAD_DIM)]
        return jnp.concatenate([_rms(c, gain) for c in heads], axis=-1)

    @pl.when(n < nblk)
    def _():
        q_ref[...] = head_norm(gq_ref[...]).astype(bf16)

    @pl.when((n >= nblk) & (n < 2 * nblk))
    def _():
        k_ref[...] = head_norm(gk_ref[...])

    @pl.when(n >= 2 * nblk)
    def _():
        v_ref[...] = r


def _qkv_proj(x, g, w_qkv, g_q, g_k, *, tm, tn=512):
    m, d = x.shape
    nblk = d // tn
    vmem = 2 * tm * d * 4 + tm * d * 2 + 2 * d * tn * 2 + 2 * 3 * tm * tn * 4 + 3 * tm * tn * 4

    def out_spec(which):
        return pl.BlockSpec((tm, tn), lambda i, n: (i, jnp.clip(n - which * nblk, 0, nblk - 1)))

    return pl.pallas_call(
        functools.partial(_qkv_kernel, nblk=nblk),
        out_shape=(jax.ShapeDtypeStruct((m, d), bf16), jax.ShapeDtypeStruct((m, d), f32),
                   jax.ShapeDtypeStruct((m, d), f32)),
        grid=(m // tm, 3 * nblk),
        in_specs=[pl.BlockSpec((tm, d), lambda i, n: (i, 0)),
                  pl.BlockSpec((1, d), lambda i, n: (0, 0)),
                  pl.BlockSpec((d, tn), lambda i, n: (0, n)),
                  pl.BlockSpec((1, SB_HEAD_DIM), lambda i, n: (0, 0)),
                  pl.BlockSpec((1, SB_HEAD_DIM), lambda i, n: (0, 0))],
        out_specs=(out_spec(0), out_spec(1), out_spec(2)),
        scratch_shapes=[pltpu.VMEM((tm, d), bf16)],
        compiler_params=_params(("parallel", "arbitrary"), vmem),
        name="qkv_proj",
    )(x, g.reshape(1, d), w_qkv, g_q.reshape(1, SB_HEAD_DIM), g_k.reshape(1, SB_HEAD_DIM))


def _later_matrix(tk):
    j = lax.broadcasted_iota(jnp.int32, (tk, tk), 0)
    s = lax.broadcasted_iota(jnp.int32, (tk, tk), 1)
    return jnp.where(j > s, 1.0, 0.0).astype(bf16)


def _sb_logw(z, later_m, mask):
    l = jnp.log(1.0 + jnp.exp(-jnp.abs(z)))
    sp = jnp.maximum(z, 0.0) + l
    if mask is not None:
        sp = jnp.where(mask, sp, 0.0)
    hi = sp.astype(bf16)
    lo = (sp - hi.astype(f32)).astype(bf16)
    later = (jnp.dot(hi, later_m, preferred_element_type=f32)
             + jnp.dot(lo, later_m, preferred_element_type=f32))
    return jnp.minimum(z, 0.0) - l - later, jnp.sum(sp, axis=-1, keepdims=True)


def _sb_prompt_kernel(bias_ref, q_ref, k_ref, v_ref, o_ref, kb_ref, vb_ref, *, tq):
    kb_ref[...] = k_ref[0].astype(bf16)
    vb_ref[...] = v_ref[0].astype(bf16)
    bias = bias_ref[pl.program_id(1)]
    scale = 1.0 / math.sqrt(SB_HEAD_DIM)
    later_m = _later_matrix(tq)
    r = lax.broadcasted_iota(jnp.int32, (tq, tq), 0)
    c = lax.broadcasted_iota(jnp.int32, (tq, tq), 1)
    causal = c < r
    for qi in range(q_ref.shape[1] // tq):
        q = q_ref[0, qi * tq:(qi + 1) * tq, :]
        acc = jnp.zeros((tq, SB_HEAD_DIM), f32)
        carry = None
        for kb in range(qi, -1, -1):
            rows = slice(kb * tq, (kb + 1) * tq)
            mask = causal if kb == qi else None
            z = lax.dot_general(q, kb_ref[rows, :], _NT, preferred_element_type=f32) * scale + bias
            logw, tot = _sb_logw(z, later_m, mask)
            if carry is not None:
                logw = logw - carry
            w = jnp.exp(logw)
            if mask is not None:
                w = jnp.where(mask, w, 0.0)
            acc = acc + jnp.dot(w.astype(bf16), vb_ref[rows, :], preferred_element_type=f32)
            carry = tot if carry is None else carry + tot
        o_ref[0, qi * tq:(qi + 1) * tq, :] = acc.astype(bf16)


def _sb_prompt(q, k, v, bias, *, tq):
    b, t, d = q.shape
    nh = d // SB_HEAD_DIM
    head = pl.BlockSpec((1, t, SB_HEAD_DIM), lambda i, h: (i, 0, h))
    vmem = 2 * 2 * t * SB_HEAD_DIM * 4 + 2 * 2 * t * SB_HEAD_DIM * 2 + 2 * t * SB_HEAD_DIM * 2 + 24 * tq * tq * 4
    return pl.pallas_call(
        functools.partial(_sb_prompt_kernel, tq=tq),
        out_shape=jax.ShapeDtypeStruct((b, t, d), bf16),
        grid=(b, nh),
        in_specs=[pl.BlockSpec(memory_space=pltpu.SMEM), head, head, head],
        out_specs=head,
        scratch_shapes=[pltpu.VMEM((t, SB_HEAD_DIM), bf16), pltpu.VMEM((t, SB_HEAD_DIM), bf16)],
        compiler_params=_params(("parallel", "parallel"), vmem),
        name="sb_prompt",
    )(bias, q, k, v)


def _sb_decode_kernel(pt_ref, q_ref, kn_ref, vn_ref, *rest, ts, nh, npg):
    kp_refs, vp_refs = rest[:npg], rest[npg:2 * npg]
    (bias_ref, o_ref, qbd_ref, qstage_ref, acc_ref, carry_ref, k2d_ref, v2d_ref, later_ref) = rest[2 * npg:]
    i = pl.program_id(1)
    nrow = nh * ts
    dh = SB_HEAD_DIM
    scale = 1.0 / math.sqrt(dh)

    def block(k2d, v2d, later_m, mask):
        z = lax.dot_general(qbd_ref[...], k2d, _NT, preferred_element_type=f32) * scale + bias_ref[...]
        logw, tot = _sb_logw(z, later_m, mask)
        w = jnp.exp(logw - carry_ref[...])
        if mask is not None:
            w = jnp.where(mask, w, 0.0)
        carry_ref[...] += tot
        out = jnp.dot(w.astype(bf16), v2d, preferred_element_type=f32)
        for h in range(nh):
            acc_ref[h * ts:(h + 1) * ts, :] += out[h * ts:(h + 1) * ts, h * dh:(h + 1) * dh]

    @pl.when(i == 0)
    def _():
        qstage_ref[...] = jnp.zeros_like(qstage_ref)
        for h in range(nh):
            qstage_ref[h * ts:(h + 1) * ts, h * dh:(h + 1) * dh] = q_ref[0, :, h * dh:(h + 1) * dh]
        qbd_ref[...] = qstage_ref[...].astype(bf16)
        acc_ref[...] = jnp.zeros_like(acc_ref)
        carry_ref[...] = jnp.zeros_like(carry_ref)
        later_ref[...] = _later_matrix(later_ref.shape[0])
        nk = kn_ref.shape[1]
        t_of_row = lax.broadcasted_iota(jnp.int32, (nrow, nk), 0) % ts
        j = lax.broadcasted_iota(jnp.int32, (nrow, nk), 1)
        block(kn_ref[0].astype(bf16), vn_ref[0].astype(bf16), later_ref[0:nk, 0:nk], j < t_of_row)

    @pl.when(i > 0)
    def _():
        for s in range(npg):
            for h in range(nh):
                head_rows = pl.ds(h, PAGE_SIZE, stride=nh)
                k2d_ref[s * PAGE_SIZE:(s + 1) * PAGE_SIZE, h * dh:(h + 1) * dh] = kp_refs[s][head_rows, :].astype(bf16)
                v2d_ref[s * PAGE_SIZE:(s + 1) * PAGE_SIZE, h * dh:(h + 1) * dh] = vp_refs[s][head_rows, :].astype(bf16)
        block(k2d_ref[...], v2d_ref[...], later_ref[...], None)

    @pl.when(i == pl.num_programs(1) - 1)
    def _():
        for h in range(nh):
            o_ref[0, :, h * dh:(h + 1) * dh] = acc_ref[h * ts:(h + 1) * ts, :]


def _sb_decode(q, k_new, v_new, cache_k, cache_v, page_table, bias_rows, *, npg):
    b, ts, d = q.shape
    nh = d // SB_HEAD_DIM
    n_pages = page_table.shape[1]
    nrow = nh * ts
    page_rows = PAGE_SIZE * nh
    nkeys = npg * PAGE_SIZE
    per_b = lambda i, j, pt: (i, 0, 0)

    def page_spec(s):
        return pl.BlockSpec((page_rows, SB_HEAD_DIM),
                            lambda i, j, pt: (pt[i, n_pages - jnp.maximum(j, 1) * npg + s], 0))

    pages = [page_spec(s) for s in range(npg)]
    vmem = (2 * 2 * npg * PAGE_SIZE * d * 4 + 2 * 2 * PAGE_SIZE * d * 4 + 2 * nkeys * d * 2
            + nrow * d * 6 + nkeys * nkeys * 2 + 4 * nrow * d * 4)
    return pl.pallas_call(
        functools.partial(_sb_decode_kernel, ts=ts, nh=nh, npg=npg),
        out_shape=jax.ShapeDtypeStruct((b, ts, d), f32),
        grid_spec=pltpu.PrefetchScalarGridSpec(
            num_scalar_prefetch=1,
            grid=(b, n_pages // npg + 1),
            in_specs=[pl.BlockSpec((1, ts, d), per_b),
                      pl.BlockSpec((1, PAGE_SIZE, d), per_b),
                      pl.BlockSpec((1, PAGE_SIZE, d), per_b),
                      *pages, *pages,
                      pl.BlockSpec((nrow, 1), lambda i, j, pt: (0, 0))],
            out_specs=pl.BlockSpec((1, ts, d), per_b),
            scratch_shapes=[pltpu.VMEM((nrow, d), bf16), pltpu.VMEM((nrow, d), f32),
                            pltpu.VMEM((nrow, SB_HEAD_DIM), f32), pltpu.VMEM((nrow, 1), f32),
                            pltpu.VMEM((nkeys, d), bf16), pltpu.VMEM((nkeys, d), bf16),
                            pltpu.VMEM((nkeys, nkeys), bf16)]),
        compiler_params=_params(("parallel", "arbitrary"), vmem),
        name="sb_decode",
    )(page_table, q, k_new, v_new, *([cache_k] * npg), *([cache_v] * npg), bias_rows)


def kernel(x_prompt, x_sample, cache_pool, state_ssm_re, state_ssm_im, cache_k, cache_v, page_table,
           norm_mix, norm_ffn, w_ffn_gate, w_ffn_up, w_ffn_down, w_pool, pool_scale,
           ssm_a_re, ssm_a_im, ssm_b_re, ssm_b_im, ssm_c_re, ssm_c_im, ssm_d, ssm_log_dt,
           w_glu_a, w_glu_b, w_qkv, w_o, sb_q_norm, sb_k_norm, sb_bias):
    bp, tp, d = x_prompt.shape
    bs, ts, _ = x_sample.shape
    depth = norm_mix.shape[0]
    nh = d // SB_HEAD_DIM
    mp, ms = bp * tp, bs * ts
    past_len = page_table.shape[1] * PAGE_SIZE
    n_phys = cache_k.shape[1]
    tm = TM

    xp, xs = x_prompt, x_sample
    pool_p, pool_s = [], []
    ssm_p, ssm_s = [], []
    kv_p, kv_s = [], []
    for i in range(depth):
        kind = i % N_MIXERS
        j = i // N_MIXERS
        if kind == 0:
            w = w_pool[j].astype(bf16)
            zero_prev = jnp.zeros((bp, POOL_HALO, d), f32)
            prev = jnp.pad(cache_pool[j], ((0, 0), (1, 0), (0, 0)))
            xp, buf_p = _pool_layer(xp, zero_prev, norm_mix[i], w, pool_scale[j], tt=POOL_TT, start_pos=0)
            xs, buf_s = _pool_layer(xs, prev, norm_mix[i], w, pool_scale[j], tt=ts, start_pos=past_len)
            pool_p.append(buf_p[:, 1:])
            pool_s.append(buf_s[:, 1:])
        elif kind == 1:
            a_re, a_im, bblk, cblk = _s5_discretise(ssm_a_re[j], ssm_a_im[j], ssm_b_re[j], ssm_b_im[j],
                                                    ssm_c_re[j], ssm_c_im[j], ssm_log_dt[j])
            dvec = ssm_d[j].reshape(d // LANES, 1, LANES)
            zero_state = jnp.zeros((d // LANES, bp, S5_PACK * S5_STATE), f32)
            gp, sr_p, si_p = _s5_scan(xp, norm_mix[i], zero_state, zero_state, a_re, a_im, bblk, cblk, dvec,
                                      tt=S5_TT)
            gs, sr_s, si_s = _s5_scan(xs, norm_mix[i], _pack_state(state_ssm_re[j]), _pack_state(state_ssm_im[j]),
                                      a_re, a_im, bblk, cblk, dvec, tt=ts)
            wa, wb = w_glu_a[j].astype(bf16), w_glu_b[j].astype(bf16)
            xp = _glu_residual(xp.reshape(mp, d), gp.reshape(mp, d), wa, wb, tm=TM_GLU).reshape(bp, tp, d)
            xs = _glu_residual(xs.reshape(ms, d), gs.reshape(ms, d), wa, wb, tm=ms).reshape(bs, ts, d)
            ssm_p.append((_unpack_state(sr_p), _unpack_state(si_p)))
            ssm_s.append((_unpack_state(sr_s), _unpack_state(si_s)))
        else:
            wqkv, wo = w_qkv[j].astype(bf16), w_o[j].astype(bf16)
            q_p, k_p, v_p = _qkv_proj(xp.reshape(mp, d), norm_mix[i], wqkv, sb_q_norm[j], sb_k_norm[j], tm=tm)
            q_s, k_s, v_s = _qkv_proj(xs.reshape(ms, d), norm_mix[i], wqkv, sb_q_norm[j], sb_k_norm[j], tm=ms)
            o_p = _sb_prompt(q_p.reshape(bp, tp, d), k_p.reshape(bp, tp, d), v_p.reshape(bp, tp, d), sb_bias[j],
                             tq=TQ)
            pad_new = lambda a: jnp.pad(a.reshape(bs, ts, d), ((0, 0), (0, PAGE_SIZE - ts), (0, 0)))
            o_s = _sb_decode(q_s.reshape(bs, ts, d).astype(f32), pad_new(k_s), pad_new(v_s),
                             cache_k.reshape(-1, SB_HEAD_DIM), cache_v.reshape(-1, SB_HEAD_DIM),
                             page_table + j * n_phys, jnp.repeat(sb_bias[j], ts).reshape(nh * ts, 1),
                             npg=DECODE_PAGES)
            xp = _proj_residual(xp.reshape(mp, d), o_p.reshape(mp, d), wo, tm=tm).reshape(bp, tp, d)
            xs = _proj_residual(xs.reshape(ms, d), o_s.reshape(ms, d).astype(bf16), wo, tm=ms).reshape(bs, ts, d)
            kv_p.append((k_p.reshape(bp, tp, nh, SB_HEAD_DIM), v_p.reshape(bp, tp, nh, SB_HEAD_DIM)))
            kv_s.append((k_s.reshape(bs, ts, nh, SB_HEAD_DIM), v_s.reshape(bs, ts, nh, SB_HEAD_DIM)))
        wg, wu, wd = w_ffn_gate[i].astype(bf16), w_ffn_up[i].astype(bf16), w_ffn_down[i].astype(bf16)
        xp = _ffn(xp.reshape(mp, d), norm_ffn[i], wg, wu, wd, tm=tm).reshape(bp, tp, d)
        xs = _ffn(xs.reshape(ms, d), norm_ffn[i], wg, wu, wd, tm=ms).reshape(bs, ts, d)

    stack = lambda items: jnp.stack(items, axis=0)
    return (xp, xs, stack(pool_p), stack(pool_s),
            stack([s[0] for s in ssm_p]), stack([s[1] for s in ssm_p]),
            stack([s[0] for s in ssm_s]), stack([s[1] for s in ssm_s]),
            stack([kv[0] for kv in kv_p]), stack([kv[1] for kv in kv_p]),
            stack([kv[0] for kv in kv_s]), stack([kv[1] for kv in kv_s]))
```

```python
import functools
import math

import jax
import jax.numpy as jnp
from jax import lax
from jax.experimental import pallas as pl
from jax.experimental.pallas import tpu as pltpu

f32 = jnp.float32
bf16 = jnp.bfloat16

RMS_EPS = 1e-6
N_MIXERS = 3
POOL_WINDOWS = (2, 4, 8, 16)
POOL_HALO = 16
S5_GROUP_SIZE = 16
S5_STATE = 64
S5_PACK = 8
SB_HEAD_DIM = 128
PAGE_SIZE = 128
LANES = 128
MIB = 1024 * 1024
VMEM_CAP = 56 * MIB

TM = 512
TM_FFN = 1024
TN_QKV = 1024
TM_GLU = 256
POOL_TT = 512
S5_TT = 128
TQ = 512
SB_CHUNK = 256
SCAN_UNROLL = 8
DECODE_PAGES = 4
LOG2E = 1.4426950408889634

_NT = (((1,), (1,)), ((), ()))


def _params(sem, vmem_bytes):
    return pltpu.CompilerParams(dimension_semantics=sem,
                                vmem_limit_bytes=int(min(VMEM_CAP, vmem_bytes + 12 * MIB)))


def _rms(x, g):
    return x * lax.rsqrt(jnp.mean(x * x, axis=-1, keepdims=True) + RMS_EPS) * g


def _resident(shape):
    nd = len(shape)
    return pl.BlockSpec(shape, lambda *_: (0,) * nd, pipeline_mode=pl.Buffered(1))


def _ffn_kernel(x_ref, g_ref, wg_ref, wu_ref, wd_ref, o_ref, h_ref):
    f = pl.program_id(1)

    @pl.when(f == 0)
    def _():
        x = x_ref[...]
        h_ref[...] = _rms(x, g_ref[...]).astype(bf16)
        o_ref[...] = x

    h = h_ref[...]
    a = jnp.dot(h, wg_ref[...], preferred_element_type=f32)
    b = jnp.dot(h, wu_ref[...], preferred_element_type=f32)
    act = (a * jax.nn.sigmoid(a) * b).astype(bf16)
    o_ref[...] += jnp.dot(act, wd_ref[...], preferred_element_type=f32)


def _ffn(x, g, wg, wu, wd, layer, *, tm, tf=512):
    m, d = x.shape
    nf = wg.shape[2] // tf
    vmem = 3 * tm * d * 4 + tm * d * 2 + 2 * 3 * d * tf * 2 + 3 * tm * tf * 4
    return pl.pallas_call(
        _ffn_kernel,
        out_shape=jax.ShapeDtypeStruct((m, d), f32),
        grid=(m // tm, nf),
        in_specs=[pl.BlockSpec((tm, d), lambda i, j: (i, 0), pipeline_mode=pl.Buffered(1)),
                  pl.BlockSpec((1, d), lambda i, j: (0, 0)),
                  pl.BlockSpec((None, d, tf), lambda i, j: (layer, 0, j)),
                  pl.BlockSpec((None, d, tf), lambda i, j: (layer, 0, j)),
                  pl.BlockSpec((None, tf, d), lambda i, j: (layer, j, 0))],
        out_specs=pl.BlockSpec((tm, d), lambda i, j: (i, 0)),
        scratch_shapes=[pltpu.VMEM((tm, d), bf16)],
        compiler_params=_params(("parallel", "arbitrary"), vmem),
        name="ffn",
    )(x, g.reshape(1, d), wg, wu, wd)


def _proj_kernel(x_ref, a_ref, w_ref, o_ref):
    o_ref[...] = x_ref[...] + jnp.dot(a_ref[...], w_ref[...], preferred_element_type=f32)


def _proj_residual(x, a, w, *, tm):
    m, d = x.shape
    k = a.shape[1]
    vmem = 2 * 2 * tm * d * 4 + 2 * tm * k * 2 + k * d * 2 + tm * d * 4
    return pl.pallas_call(
        _proj_kernel,
        out_shape=jax.ShapeDtypeStruct((m, d), f32),
        grid=(m // tm,),
        in_specs=[pl.BlockSpec((tm, d), lambda i: (i, 0)),
                  pl.BlockSpec((tm, k), lambda i: (i, 0)),
                  _resident((k, d))],
        out_specs=pl.BlockSpec((tm, d), lambda i: (i, 0)),
        compiler_params=_params(("parallel",), vmem),
        name="proj_residual",
    )(x, a, w)


def _glu_kernel(x_ref, a_ref, wa_ref, wb_ref, o_ref):
    a = a_ref[...]
    p = jnp.dot(a, wa_ref[...], preferred_element_type=f32)
    q = jnp.dot(a, wb_ref[...], preferred_element_type=f32)
    o_ref[...] = x_ref[...] + p * jax.nn.sigmoid(q)


def _glu_residual(x, a, wa, wb, *, tm):
    m, d = x.shape
    k = a.shape[1]
    vmem = 2 * 2 * tm * d * 4 + 2 * tm * k * 2 + 2 * k * d * 2 + 3 * tm * d * 4
    return pl.pallas_call(
        _glu_kernel,
        out_shape=jax.ShapeDtypeStruct((m, d), f32),
        grid=(m // tm,),
        in_specs=[pl.BlockSpec((tm, d), lambda i: (i, 0)),
                  pl.BlockSpec((tm, k), lambda i: (i, 0)),
                  _resident((k, d)), _resident((k, d))],
        out_specs=pl.BlockSpec((tm, d), lambda i: (i, 0)),
        compiler_params=_params(("parallel",), vmem),
        name="glu_residual",
    )(x, a, wa, wb)


def _pool_kernel(x_ref, prev_ref, g_ref, w_ref, sc_ref, o_ref, buf_ref, ext_ref, *, tt, start_pos):
    t = pl.program_id(1)
    halo = POOL_HALO

    @pl.when(t == 0)
    def _():
        ext_ref[0:halo, :] = prev_ref[0]

    @pl.when(t > 0)
    def _():
        ext_ref[0:halo, :] = ext_ref[tt:tt + halo, :]

    x = x_ref[0]
    h = _rms(x, g_ref[...])
    ext_ref[halo:halo + tt, :] = h
    buf_ref[0] = ext_ref[tt:tt + halo, :]

    gd = x.shape[1] // len(POOL_WINDOWS)
    pos = start_pos + t * tt + lax.broadcasted_iota(jnp.int32, (tt, 1), 0)
    for gi, w in enumerate(POOL_WINDOWS):
        sl = slice(gi * gd, (gi + 1) * gd)
        win = h[:, sl]
        for i in range(1, w):
            win = win + ext_ref[halo - i:halo - i + tt, sl]
        cnt = jnp.minimum(pos + 1, w).astype(f32)
        p = win / cnt - h[:, sl]
        y = jnp.dot(p.astype(bf16), w_ref[gi], preferred_element_type=f32)
        o_ref[0, :, sl] = x[:, sl] + y * sc_ref[:, sl]


def _pool_layer(x, prev16, g, w_pool, scale, *, tt, start_pos):
    b, t, d = x.shape
    ng, gd = w_pool.shape[0], w_pool.shape[1]
    vmem = 2 * 2 * tt * d * 4 + (tt + POOL_HALO) * d * 4 + ng * gd * gd * 2 + 4 * tt * d * 4
    return pl.pallas_call(
        functools.partial(_pool_kernel, tt=tt, start_pos=start_pos),
        out_shape=(jax.ShapeDtypeStruct((b, t, d), f32), jax.ShapeDtypeStruct((b, POOL_HALO, d), f32)),
        grid=(b, t // tt),
        in_specs=[pl.BlockSpec((1, tt, d), lambda i, j: (i, j, 0)),
                  pl.BlockSpec((1, POOL_HALO, d), lambda i, j: (i, 0, 0)),
                  pl.BlockSpec((1, d), lambda i, j: (0, 0)),
                  _resident((ng, gd, gd)),
                  pl.BlockSpec((1, d), lambda i, j: (0, 0))],
        out_specs=(pl.BlockSpec((1, tt, d), lambda i, j: (i, j, 0)),
                   pl.BlockSpec((1, POOL_HALO, d), lambda i, j: (i, 0, 0))),
        scratch_shapes=[pltpu.VMEM((tt + POOL_HALO, d), f32)],
        compiler_params=_params(("parallel", "arbitrary"), vmem),
        name="pool_layer",
    )(x, prev16, g.reshape(1, d), w_pool, scale.reshape(1, d))


def _s5_kernel(x_ref, g_ref, h0r_ref, h0i_ref, are_ref, aim_ref, bblk_ref, cblk_ref, d_ref,
               gout_ref, sr_ref, si_ref, u_ref, bu_ref, gel_ref, *, nb, tt):
    t = pl.program_id(0)
    npack = u_ref.shape[0]
    nsl = (S5_PACK * S5_STATE) // LANES

    @pl.when(t == 0)
    def _():
        sr_ref[...] = h0r_ref[...]
        si_ref[...] = h0i_ref[...]

    g = g_ref[...]
    for b in range(nb):
        u = _rms(x_ref[b], g)
        for p in range(npack):
            u_ref.at[p][pl.ds(b, tt, stride=nb), :] = u[:, p * LANES:(p + 1) * LANES]

    def pack_body(p, carry):
        u = u_ref[p]
        bu = jnp.dot(u.astype(bf16), bblk_ref[p], preferred_element_type=f32)
        for j in range(2 * nsl):
            bu_ref[j] = bu[:, j * LANES:(j + 1) * LANES]
        a_re = are_ref[p]
        a_im = aim_ref[p]
        ar = [jnp.broadcast_to(a_re[:, j * LANES:(j + 1) * LANES], (nb, LANES)) for j in range(nsl)]
        ai = [jnp.broadcast_to(a_im[:, j * LANES:(j + 1) * LANES], (nb, LANES)) for j in range(nsl)]
        s0r = sr_ref[p]
        s0i = si_ref[p]
        init = (tuple(s0r[:, j * LANES:(j + 1) * LANES] for j in range(nsl)),
                tuple(s0i[:, j * LANES:(j + 1) * LANES] for j in range(nsl)))

        def step(k, s):
            s_re, s_im = s
            new_re, new_im = [], []
            for j in range(nsl):
                rows = pl.ds(pl.multiple_of(k * nb, nb), nb)
                b_re = bu_ref.at[j][rows, :]
                b_im = bu_ref.at[nsl + j][rows, :]
                n_re = ar[j] * s_re[j] - ai[j] * s_im[j] + b_re
                n_im = ar[j] * s_im[j] + ai[j] * s_re[j] + b_im
                bu_ref.at[j][rows, :] = n_re
                bu_ref.at[nsl + j][rows, :] = n_im
                new_re.append(n_re)
                new_im.append(n_im)
            return tuple(new_re), tuple(new_im)

        s_re, s_im = lax.fori_loop(0, tt, step, init, unroll=SCAN_UNROLL)
        sr_ref[p] = jnp.concatenate(s_re, axis=-1)
        si_ref[p] = jnp.concatenate(s_im, axis=-1)
        states = jnp.concatenate([bu_ref[j] for j in range(2 * nsl)], axis=-1).astype(bf16)
        y = jnp.dot(states, cblk_ref[p], preferred_element_type=f32) + d_ref[p] * u
        gel_ref[p] = 0.5 * y * (1.0 + lax.erf(y * math.sqrt(0.5)))
        return carry

    lax.fori_loop(0, npack, pack_body, 0)
    for b in range(nb):
        for p in range(npack):
            gout_ref[b, :, p * LANES:(p + 1) * LANES] = gel_ref.at[p][pl.ds(b, tt, stride=nb), :].astype(bf16)


def _s5_scan(x, g, h0_re, h0_im, a_re, a_im, bblk, cblk, dvec, *, tt):
    b, t, d = x.shape
    npack = d // LANES
    sw = S5_PACK * S5_STATE
    rows = b * tt
    vmem = (2 * rows * d * 4 + rows * d * 4 + rows * 2 * sw * 4 + rows * d * 4 + 2 * rows * d * 2
            + 2 * npack * LANES * 2 * sw * 2 + 3 * rows * 2 * sw * 4)
    st_spec = pl.BlockSpec((npack, b, sw), lambda i: (0, 0, 0))
    return pl.pallas_call(
        functools.partial(_s5_kernel, nb=b, tt=tt),
        out_shape=(jax.ShapeDtypeStruct((b, t, d), bf16),
                   jax.ShapeDtypeStruct((npack, b, sw), f32),
                   jax.ShapeDtypeStruct((npack, b, sw), f32)),
        grid=(t // tt,),
        in_specs=[pl.BlockSpec((b, tt, d), lambda i: (0, i, 0)),
                  pl.BlockSpec((1, d), lambda i: (0, 0)),
                  st_spec, st_spec,
                  _resident((npack, 1, sw)), _resident((npack, 1, sw)),
                  _resident((npack, LANES, 2 * sw)), _resident((npack, 2 * sw, LANES)),
                  _resident((npack, 1, LANES))],
        out_specs=(pl.BlockSpec((b, tt, d), lambda i: (0, i, 0)), st_spec, st_spec),
        scratch_shapes=[pltpu.VMEM((npack, rows, LANES), f32),
                        pltpu.VMEM((2 * sw // LANES, rows, LANES), f32),
                        pltpu.VMEM((npack, rows, LANES), f32)],
        compiler_params=_params(("arbitrary",), vmem),
        name="s5_scan",
    )(x, g.reshape(1, d), h0_re, h0_im, a_re, a_im, bblk, cblk, dvec)


def _s5_discretise(a_re, a_im, b_re, b_im, c_re, c_im, log_dt):
    g, n = a_re.shape
    c = b_re.shape[-1]
    npack = g // S5_PACK
    dt = jnp.exp(log_dt)[:, None]
    mag = jnp.exp(dt * a_re)
    abar_re = mag * jnp.cos(dt * a_im)
    abar_im = mag * jnp.sin(dt * a_im)
    den = a_re * a_re + a_im * a_im
    nr = abar_re - 1.0
    f_re = (nr * a_re + abar_im * a_im) / den
    f_im = (abar_im * a_re - nr * a_im) / den
    bbar_re = f_re[..., None] * b_re - f_im[..., None] * b_im
    bbar_im = f_re[..., None] * b_im + f_im[..., None] * b_re
    eye = jnp.eye(S5_PACK, dtype=f32)

    def in_blocks(m):
        m = jnp.swapaxes(m, 1, 2).reshape(npack, S5_PACK, c, n)
        return jnp.einsum('pgcn,gh->pgchn', m, eye).reshape(npack, S5_PACK * c, S5_PACK * n)

    def out_blocks(m):
        m = jnp.swapaxes(m, 1, 2).reshape(npack, S5_PACK, n, c)
        return jnp.einsum('pgnc,gh->pgnhc', m, eye).reshape(npack, S5_PACK * n, S5_PACK * c)

    bblk = jnp.concatenate([in_blocks(bbar_re), in_blocks(bbar_im)], axis=2).astype(bf16)
    cblk = jnp.concatenate([out_blocks(c_re), -out_blocks(c_im)], axis=1).astype(bf16)
    pack = lambda v: v.reshape(npack, 1, S5_PACK * n)
    return pack(abar_re), pack(abar_im), bblk, cblk


def _pack_state(s):
    b, g, n = s.shape
    return jnp.swapaxes(s.reshape(b, g // S5_PACK, S5_PACK * n), 0, 1)


def _unpack_state(s):
    npack, b, sw = s.shape
    return jnp.swapaxes(s, 0, 1).reshape(b, npack * S5_PACK, S5_STATE)


def _qkv_kernel(x_ref, g_ref, w_ref, gq_ref, gk_ref, q_ref, k_ref, v_ref, h_ref, *, nblk):
    n = pl.program_id(1)

    @pl.when(n == 0)
    def _():
        h_ref[...] = _rms(x_ref[...], g_ref[...]).astype(bf16)

    r = jnp.dot(h_ref[...], w_ref[...], preferred_element_type=f32)

    def head_norm(gain):
        heads = [r[:, i * SB_HEAD_DIM:(i + 1) * SB_HEAD_DIM] for i in range(r.shape[1] // SB_HEAD_DIM)]
        return jnp.concatenate([_rms(c, gain) for c in heads], axis=-1)

    @pl.when(n < nblk)
    def _():
        q_ref[...] = head_norm(gq_ref[...]).astype(bf16)

    @pl.when((n >= nblk) & (n < 2 * nblk))
    def _():
        k_ref[...] = head_norm(gk_ref[...])

    @pl.when(n >= 2 * nblk)
    def _():
        v_ref[...] = r


def _qkv_proj(x, g, w_qkv, g_q, g_k, *, tm, tn):
    m, d = x.shape
    nblk = d // tn
    vmem = 2 * tm * d * 4 + tm * d * 2 + 2 * d * tn * 2 + 2 * 3 * tm * tn * 4 + 3 * tm * tn * 4

    def out_spec(which):
        return pl.BlockSpec((tm, tn), lambda i, n: (i, jnp.clip(n - which * nblk, 0, nblk - 1)))

    return pl.pallas_call(
        functools.partial(_qkv_kernel, nblk=nblk),
        out_shape=(jax.ShapeDtypeStruct((m, d), bf16), jax.ShapeDtypeStruct((m, d), f32),
                   jax.ShapeDtypeStruct((m, d), f32)),
        grid=(m // tm, 3 * nblk),
        in_specs=[pl.BlockSpec((tm, d), lambda i, n: (i, 0)),
                  pl.BlockSpec((1, d), lambda i, n: (0, 0)),
                  pl.BlockSpec((d, tn), lambda i, n: (0, n)),
                  pl.BlockSpec((1, SB_HEAD_DIM), lambda i, n: (0, 0)),
                  pl.BlockSpec((1, SB_HEAD_DIM), lambda i, n: (0, 0))],
        out_specs=(out_spec(0), out_spec(1), out_spec(2)),
        scratch_shapes=[pltpu.VMEM((tm, d), bf16)],
        compiler_params=_params(("parallel", "arbitrary"), vmem),
        name="qkv_proj",
    )(x, g.reshape(1, d), w_qkv, g_q.reshape(1, SB_HEAD_DIM), g_k.reshape(1, SB_HEAD_DIM))


def _later_matrix(tk):
    j = lax.broadcasted_iota(jnp.int32, (tk, tk), 0)
    s = lax.broadcasted_iota(jnp.int32, (tk, tk), 1)
    return jnp.where(j > s, 1.0, 0.0).astype(bf16)


def _sb_logw2(z2, later_m, mask):
    chunk = later_m.shape[0]
    l = jnp.log(1.0 + jnp.exp2(-jnp.abs(z2))) * LOG2E
    sp = jnp.maximum(z2, 0.0) + l
    if mask is not None:
        sp = jnp.where(mask, sp, 0.0)
    sp16 = sp.astype(bf16)
    nchunk = z2.shape[1] // chunk
    later, total = [None] * nchunk, None
    for c in range(nchunk - 1, -1, -1):
        cols = slice(c * chunk, (c + 1) * chunk)
        in_chunk = jnp.dot(sp16[:, cols], later_m, preferred_element_type=f32)
        later[c] = in_chunk if total is None else in_chunk + total
        rowsum = jnp.sum(sp[:, cols], axis=-1, keepdims=True)
        total = rowsum if total is None else total + rowsum
    later = later[0] if nchunk == 1 else jnp.concatenate(later, axis=1)
    return jnp.minimum(z2, 0.0) - l - later, total


def _sb_prompt_kernel(bias_ref, q_ref, k_ref, v_ref, o_ref, kb_ref, vb_ref, *, tq):
    kb_ref[...] = k_ref[0].astype(bf16)
    vb_ref[...] = v_ref[0].astype(bf16)
    bias2 = bias_ref[pl.program_id(1)] * LOG2E
    scale2 = LOG2E / math.sqrt(SB_HEAD_DIM)
    later_m = _later_matrix(min(SB_CHUNK, tq))
    r = lax.broadcasted_iota(jnp.int32, (tq, tq), 0)
    c = lax.broadcasted_iota(jnp.int32, (tq, tq), 1)
    causal = c < r
    for qi in range(q_ref.shape[1] // tq):
        q = q_ref[0, qi * tq:(qi + 1) * tq, :]
        acc = jnp.zeros((tq, SB_HEAD_DIM), f32)
        carry = None
        for kb in range(qi, -1, -1):
            rows = slice(kb * tq, (kb + 1) * tq)
            mask = causal if kb == qi else None
            z2 = lax.dot_general(q, kb_ref[rows, :], _NT, preferred_element_type=f32) * scale2 + bias2
            logw, tot = _sb_logw2(z2, later_m, mask)
            if carry is not None:
                logw = logw - carry
            w = jnp.exp2(logw)
            if mask is not None:
                w = jnp.where(mask, w, 0.0)
            acc = acc + jnp.dot(w.astype(bf16), vb_ref[rows, :], preferred_element_type=f32)
            carry = tot if carry is None else carry + tot
        o_ref[0, qi * tq:(qi + 1) * tq, :] = acc.astype(bf16)


def _sb_prompt(q, k, v, bias, *, tq):
    b, t, d = q.shape
    nh = d // SB_HEAD_DIM
    head = pl.BlockSpec((1, t, SB_HEAD_DIM), lambda i, h: (i, 0, h))
    vmem = 2 * 2 * t * SB_HEAD_DIM * 4 + 2 * 2 * t * SB_HEAD_DIM * 2 + 2 * t * SB_HEAD_DIM * 2 + 24 * tq * tq * 4
    return pl.pallas_call(
        functools.partial(_sb_prompt_kernel, tq=tq),
        out_shape=jax.ShapeDtypeStruct((b, t, d), bf16),
        grid=(b, nh),
        in_specs=[pl.BlockSpec(memory_space=pltpu.SMEM), head, head, head],
        out_specs=head,
        scratch_shapes=[pltpu.VMEM((t, SB_HEAD_DIM), bf16), pltpu.VMEM((t, SB_HEAD_DIM), bf16)],
        compiler_params=_params(("parallel", "parallel"), vmem),
        name="sb_prompt",
    )(bias, q, k, v)


def _sb_decode_kernel(pt_ref, q_ref, kn_ref, vn_ref, *rest, ts, nh, npg):
    kp_refs, vp_refs = rest[:npg], rest[npg:2 * npg]
    (bias_ref, o_ref, qbd_ref, qstage_ref, acc_ref, carry_ref, k2d_ref, v2d_ref, later_ref) = rest[2 * npg:]
    i = pl.program_id(1)
    nrow = nh * ts
    dh = SB_HEAD_DIM
    scale2 = LOG2E / math.sqrt(dh)

    def regroup(page_ref, dst_ref, slot):
        for h in range(nh):
            head_rows = pl.ds(h, PAGE_SIZE, stride=nh)
            dst_ref[slot * PAGE_SIZE:(slot + 1) * PAGE_SIZE, h * dh:(h + 1) * dh] = page_ref[head_rows, :].astype(bf16)

    def block(k2d, v2d, later_m, mask):
        z2 = (lax.dot_general(qbd_ref[...], k2d, _NT, preferred_element_type=f32) * scale2
              + bias_ref[...] * LOG2E)
        logw, tot = _sb_logw2(z2, later_m, mask)
        w = jnp.exp2(logw - carry_ref[...])
        if mask is not None:
            w = jnp.where(mask, w, 0.0)
        carry_ref[...] += tot
        out = jnp.dot(w.astype(bf16), v2d, preferred_element_type=f32)
        for h in range(nh):
            acc_ref[h * ts:(h + 1) * ts, :] += out[h * ts:(h + 1) * ts, h * dh:(h + 1) * dh]

    @pl.when(i == 0)
    def _():
        qstage_ref[...] = jnp.zeros_like(qstage_ref)
        for h in range(nh):
            qstage_ref[h * ts:(h + 1) * ts, h * dh:(h + 1) * dh] = q_ref[0, :, h * dh:(h + 1) * dh]
        qbd_ref[...] = qstage_ref[...].astype(bf16)
        acc_ref[...] = jnp.zeros_like(acc_ref)
        carry_ref[...] = jnp.zeros_like(carry_ref)
        later_ref[...] = _later_matrix(later_ref.shape[0])
        nk = kn_ref.shape[1]
        t_of_row = lax.broadcasted_iota(jnp.int32, (nrow, nk), 0) % ts
        j = lax.broadcasted_iota(jnp.int32, (nrow, nk), 1)
        chunk = min(nk, later_ref.shape[0])
        block(kn_ref[0].astype(bf16), vn_ref[0].astype(bf16), later_ref[0:chunk, 0:chunk], j < t_of_row)

    @pl.when(i > 0)
    def _():
        for s in range(npg):
            regroup(kp_refs[s], k2d_ref, s)
            regroup(vp_refs[s], v2d_ref, s)
        block(k2d_ref[...], v2d_ref[...], later_ref[...], None)

    @pl.when(i == pl.num_programs(1) - 1)
    def _():
        for h in range(nh):
            o_ref[0, :, h * dh:(h + 1) * dh] = acc_ref[h * ts:(h + 1) * ts, :]


def _sb_decode(q, k_new, v_new, cache_k, cache_v, page_table, bias_rows, *, npg):
    b, ts, d = q.shape
    nh = d // SB_HEAD_DIM
    n_pages = page_table.shape[1]
    nrow = nh * ts
    page_rows = PAGE_SIZE * nh
    nkeys = npg * PAGE_SIZE
    chunk = min(SB_CHUNK, nkeys)
    per_b = lambda i, j, pt: (i, 0, 0)

    def page_spec(s):
        return pl.BlockSpec((page_rows, SB_HEAD_DIM),
                            lambda i, j, pt: (pt[i, n_pages - jnp.maximum(j, 1) * npg + s], 0))

    pages = [page_spec(s) for s in range(npg)]
    vmem = (2 * 2 * npg * PAGE_SIZE * d * 4 + 2 * 2 * PAGE_SIZE * d * 4 + 2 * nkeys * d * 2
            + nrow * d * 6 + chunk * chunk * 2 + 4 * nrow * d * 4)
    return pl.pallas_call(
        functools.partial(_sb_decode_kernel, ts=ts, nh=nh, npg=npg),
        out_shape=jax.ShapeDtypeStruct((b, ts, d), f32),
        grid_spec=pltpu.PrefetchScalarGridSpec(
            num_scalar_prefetch=1,
            grid=(b, n_pages // npg + 1),
            in_specs=[pl.BlockSpec((1, ts, d), per_b),
                      pl.BlockSpec((1, PAGE_SIZE, d), per_b),
                      pl.BlockSpec((1, PAGE_SIZE, d), per_b),
                      *pages, *pages,
                      pl.BlockSpec((nrow, 1), lambda i, j, pt: (0, 0))],
            out_specs=pl.BlockSpec((1, ts, d), per_b),
            scratch_shapes=[pltpu.VMEM((nrow, d), bf16), pltpu.VMEM((nrow, d), f32),
                            pltpu.VMEM((nrow, SB_HEAD_DIM), f32), pltpu.VMEM((nrow, 1), f32),
                            pltpu.VMEM((nkeys, d), bf16), pltpu.VMEM((nkeys, d), bf16),
                            pltpu.VMEM((chunk, chunk), bf16)]),
        compiler_params=_params(("parallel", "arbitrary"), vmem),
        name="sb_decode",
    )(page_table, q, k_new, v_new, *([cache_k] * npg), *([cache_v] * npg), bias_rows)


def kernel(x_prompt, x_sample, cache_pool, state_ssm_re, state_ssm_im, cache_k, cache_v, page_table,
           norm_mix, norm_ffn, w_ffn_gate, w_ffn_up, w_ffn_down, w_pool, pool_scale,
           ssm_a_re, ssm_a_im, ssm_b_re, ssm_b_im, ssm_c_re, ssm_c_im, ssm_d, ssm_log_dt,
           w_glu_a, w_glu_b, w_qkv, w_o, sb_q_norm, sb_k_norm, sb_bias):
    bp, tp, d = x_prompt.shape
    bs, ts, _ = x_sample.shape
    depth = norm_mix.shape[0]
    nh = d // SB_HEAD_DIM
    mp, ms = bp * tp, bs * ts
    past_len = page_table.shape[1] * PAGE_SIZE
    n_phys = cache_k.shape[1]
    tm = TM

    xp, xs = x_prompt, x_sample
    wg, wu, wd = w_ffn_gate.astype(bf16), w_ffn_up.astype(bf16), w_ffn_down.astype(bf16)
    pool_p, pool_s = [], []
    ssm_p, ssm_s = [], []
    kv_p, kv_s = [], []
    for i in range(depth):
        kind = i % N_MIXERS
        j = i // N_MIXERS
        if kind == 0:
            w = w_pool[j].astype(bf16)
            zero_prev = jnp.zeros((bp, POOL_HALO, d), f32)
            prev = jnp.pad(cache_pool[j], ((0, 0), (1, 0), (0, 0)))
            xp, buf_p = _pool_layer(xp, zero_prev, norm_mix[i], w, pool_scale[j], tt=POOL_TT, start_pos=0)
            xs, buf_s = _pool_layer(xs, prev, norm_mix[i], w, pool_scale[j], tt=ts, start_pos=past_len)
            pool_p.append(buf_p[:, 1:])
            pool_s.append(buf_s[:, 1:])
        elif kind == 1:
            a_re, a_im, bblk, cblk = _s5_discretise(ssm_a_re[j], ssm_a_im[j], ssm_b_re[j], ssm_b_im[j],
                                                    ssm_c_re[j], ssm_c_im[j], ssm_log_dt[j])
            dvec = ssm_d[j].reshape(d // LANES, 1, LANES)
            zero_state = jnp.zeros((d // LANES, bp, S5_PACK * S5_STATE), f32)
            gp, sr_p, si_p = _s5_scan(xp, norm_mix[i], zero_state, zero_state, a_re, a_im, bblk, cblk, dvec,
                                      tt=S5_TT)
            gs, sr_s, si_s = _s5_scan(xs, norm_mix[i], _pack_state(state_ssm_re[j]), _pack_state(state_ssm_im[j]),
                                      a_re, a_im, bblk, cblk, dvec, tt=ts)
            wa, wb = w_glu_a[j].astype(bf16), w_glu_b[j].astype(bf16)
            xp = _glu_residual(xp.reshape(mp, d), gp.reshape(mp, d), wa, wb, tm=TM_GLU).reshape(bp, tp, d)
            xs = _glu_residual(xs.reshape(ms, d), gs.reshape(ms, d), wa, wb, tm=ms).reshape(bs, ts, d)
            ssm_p.append((_unpack_state(sr_p), _unpack_state(si_p)))
            ssm_s.append((_unpack_state(sr_s), _unpack_state(si_s)))
        else:
            wqkv, wo = w_qkv[j].astype(bf16), w_o[j].astype(bf16)
            q_p, k_p, v_p = _qkv_proj(xp.reshape(mp, d), norm_mix[i], wqkv, sb_q_norm[j], sb_k_norm[j],
                                      tm=tm, tn=TN_QKV)
            q_s, k_s, v_s = _qkv_proj(xs.reshape(ms, d), norm_mix[i], wqkv, sb_q_norm[j], sb_k_norm[j],
                                      tm=ms, tn=TN_QKV)
            o_p = _sb_prompt(q_p.reshape(bp, tp, d), k_p.reshape(bp, tp, d), v_p.reshape(bp, tp, d), sb_bias[j],
                             tq=TQ)
            pad_new = lambda a: jnp.pad(a.reshape(bs, ts, d), ((0, 0), (0, PAGE_SIZE - ts), (0, 0)))
            o_s = _sb_decode(q_s.reshape(bs, ts, d).astype(f32), pad_new(k_s), pad_new(v_s),
                             cache_k.reshape(-1, SB_HEAD_DIM), cache_v.reshape(-1, SB_HEAD_DIM),
                             page_table + j * n_phys, jnp.repeat(sb_bias[j], ts).reshape(nh * ts, 1),
                             npg=DECODE_PAGES)
            xp = _proj_residual(xp.reshape(mp, d), o_p.reshape(mp, d), wo, tm=tm).reshape(bp, tp, d)
            xs = _proj_residual(xs.reshape(ms, d), o_s.reshape(ms, d).astype(bf16), wo, tm=ms).reshape(bs, ts, d)
            kv_p.append((k_p.reshape(bp, tp, nh, SB_HEAD_DIM), v_p.reshape(bp, tp, nh, SB_HEAD_DIM)))
            kv_s.append((k_s.reshape(bs, ts, nh, SB_HEAD_DIM), v_s.reshape(bs, ts, nh, SB_HEAD_DIM)))
        xp = _ffn(xp.reshape(mp, d), norm_ffn[i], wg, wu, wd, i, tm=TM_FFN).reshape(bp, tp, d)
        xs = _ffn(xs.reshape(ms, d), norm_ffn[i], wg, wu, wd, i, tm=ms).reshape(bs, ts, d)

    stack = lambda items: jnp.stack(items, axis=0)
    return (xp, xs, stack(pool_p), stack(pool_s),
            stack([s[0] for s in ssm_p]), stack([s[1] for s in ssm_p]),
            stack([s[0] for s in ssm_s]), stack([s[1] for s in ssm_s]),
            stack([kv[0] for kv in kv_p]), stack([kv[1] for kv in kv_p]),
            stack([kv[0] for kv in kv_s]), stack([kv[1] for kv in kv_s]))
```

```python
import functools
import math

import jax
import jax.numpy as jnp
from jax import lax
from jax.experimental import pallas as pl
from jax.experimental.pallas import tpu as pltpu

f32 = jnp.float32
bf16 = jnp.bfloat16

RMS_EPS = 1e-6
N_MIXERS = 3
POOL_WINDOWS = (2, 4, 8, 16)
POOL_HALO = 16
S5_GROUP_SIZE = 16
S5_STATE = 64
S5_PACK = 8
SB_HEAD_DIM = 128
PAGE_SIZE = 128
LANES = 128
MIB = 1024 * 1024
VMEM_CAP = 56 * MIB

TM = 512
TM_FFN = 512
TN_QKV = 1024
TM_GLU = 256
POOL_TT = 512
S5_TT = 128
TQ = 512
SB_CHUNK = 256
SCAN_UNROLL = 8
DECODE_PAGES = 4
LOG2E = 1.4426950408889634

_NT = (((1,), (1,)), ((), ()))


def _params(sem, vmem_bytes):
    return pltpu.CompilerParams(dimension_semantics=sem,
                                vmem_limit_bytes=int(min(VMEM_CAP, vmem_bytes + 12 * MIB)))


def _rms(x, g):
    return x * lax.rsqrt(jnp.mean(x * x, axis=-1, keepdims=True) + RMS_EPS) * g


def _resident(shape):
    nd = len(shape)
    return pl.BlockSpec(shape, lambda *_: (0,) * nd, pipeline_mode=pl.Buffered(1))


def _ffn_kernel(x_ref, g_ref, wg_ref, wu_ref, wd_ref, o_ref, h_ref):
    f = pl.program_id(1)

    @pl.when(f == 0)
    def _():
        x = x_ref[...]
        h_ref[...] = _rms(x, g_ref[...]).astype(bf16)
        o_ref[...] = x

    h = h_ref[...]
    a = jnp.dot(h, wg_ref[...], preferred_element_type=f32)
    b = jnp.dot(h, wu_ref[...], preferred_element_type=f32)
    act = (a * jax.nn.sigmoid(a) * b).astype(bf16)
    o_ref[...] += jnp.dot(act, wd_ref[...], preferred_element_type=f32)


def _ffn(x, g, wg, wu, wd, layer, *, tm, tf=512):
    m, d = x.shape
    nf = wg.shape[2] // tf
    vmem = 3 * tm * d * 4 + tm * d * 2 + 2 * 3 * d * tf * 2 + 3 * tm * tf * 4
    return pl.pallas_call(
        _ffn_kernel,
        out_shape=jax.ShapeDtypeStruct((m, d), f32),
        grid=(m // tm, nf),
        in_specs=[pl.BlockSpec((tm, d), lambda i, j: (i, 0), pipeline_mode=pl.Buffered(1)),
                  pl.BlockSpec((1, d), lambda i, j: (0, 0)),
                  pl.BlockSpec((None, d, tf), lambda i, j: (layer, 0, j)),
                  pl.BlockSpec((None, d, tf), lambda i, j: (layer, 0, j)),
                  pl.BlockSpec((None, tf, d), lambda i, j: (layer, j, 0))],
        out_specs=pl.BlockSpec((tm, d), lambda i, j: (i, 0)),
        scratch_shapes=[pltpu.VMEM((tm, d), bf16)],
        compiler_params=_params(("parallel", "arbitrary"), vmem),
        name="ffn",
    )(x, g.reshape(1, d), wg, wu, wd)


def _proj_kernel(x_ref, a_ref, w_ref, o_ref):
    o_ref[...] = x_ref[...] + jnp.dot(a_ref[...], w_ref[...], preferred_element_type=f32)


def _proj_residual(x, a, w, *, tm):
    m, d = x.shape
    k = a.shape[1]
    vmem = 2 * 2 * tm * d * 4 + 2 * tm * k * 2 + k * d * 2 + tm * d * 4
    return pl.pallas_call(
        _proj_kernel,
        out_shape=jax.ShapeDtypeStruct((m, d), f32),
        grid=(m // tm,),
        in_specs=[pl.BlockSpec((tm, d), lambda i: (i, 0)),
                  pl.BlockSpec((tm, k), lambda i: (i, 0)),
                  _resident((k, d))],
        out_specs=pl.BlockSpec((tm, d), lambda i: (i, 0)),
        compiler_params=_params(("parallel",), vmem),
        name="proj_residual",
    )(x, a, w)


def _glu_kernel(x_ref, a_ref, wa_ref, wb_ref, o_ref):
    a = a_ref[...]
    p = jnp.dot(a, wa_ref[...], preferred_element_type=f32)
    q = jnp.dot(a, wb_ref[...], preferred_element_type=f32)
    o_ref[...] = x_ref[...] + p * jax.nn.sigmoid(q)


def _glu_residual(x, a, wa, wb, *, tm):
    m, d = x.shape
    k = a.shape[1]
    vmem = 2 * 2 * tm * d * 4 + 2 * tm * k * 2 + 2 * k * d * 2 + 3 * tm * d * 4
    return pl.pallas_call(
        _glu_kernel,
        out_shape=jax.ShapeDtypeStruct((m, d), f32),
        grid=(m // tm,),
        in_specs=[pl.BlockSpec((tm, d), lambda i: (i, 0)),
                  pl.BlockSpec((tm, k), lambda i: (i, 0)),
                  _resident((k, d)), _resident((k, d))],
        out_specs=pl.BlockSpec((tm, d), lambda i: (i, 0)),
        compiler_params=_params(("parallel",), vmem),
        name="glu_residual",
    )(x, a, wa, wb)


def _pool_kernel(x_ref, prev_ref, g_ref, w_ref, sc_ref, o_ref, buf_ref, ext_ref, *, tt, start_pos):
    t = pl.program_id(1)
    halo = POOL_HALO

    @pl.when(t == 0)
    def _():
        ext_ref[0:halo, :] = prev_ref[0]

    @pl.when(t > 0)
    def _():
        ext_ref[0:halo, :] = ext_ref[tt:tt + halo, :]

    x = x_ref[0]
    h = _rms(x, g_ref[...])
    ext_ref[halo:halo + tt, :] = h
    buf_ref[0] = ext_ref[tt:tt + halo, :]

    gd = x.shape[1] // len(POOL_WINDOWS)
    pos = start_pos + t * tt + lax.broadcasted_iota(jnp.int32, (tt, 1), 0)
    for gi, w in enumerate(POOL_WINDOWS):
        sl = slice(gi * gd, (gi + 1) * gd)
        win = h[:, sl]
        for i in range(1, w):
            win = win + ext_ref[halo - i:halo - i + tt, sl]
        cnt = jnp.minimum(pos + 1, w).astype(f32)
        p = win / cnt - h[:, sl]
        y = jnp.dot(p.astype(bf16), w_ref[gi], preferred_element_type=f32)
        o_ref[0, :, sl] = x[:, sl] + y * sc_ref[:, sl]


def _pool_layer(x, prev16, g, w_pool, scale, *, tt, start_pos):
    b, t, d = x.shape
    ng, gd = w_pool.shape[0], w_pool.shape[1]
    vmem = 2 * 2 * tt * d * 4 + (tt + POOL_HALO) * d * 4 + ng * gd * gd * 2 + 4 * tt * d * 4
    return pl.pallas_call(
        functools.partial(_pool_kernel, tt=tt, start_pos=start_pos),
        out_shape=(jax.ShapeDtypeStruct((b, t, d), f32), jax.ShapeDtypeStruct((b, POOL_HALO, d), f32)),
        grid=(b, t // tt),
        in_specs=[pl.BlockSpec((1, tt, d), lambda i, j: (i, j, 0)),
                  pl.BlockSpec((1, POOL_HALO, d), lambda i, j: (i, 0, 0)),
                  pl.BlockSpec((1, d), lambda i, j: (0, 0)),
                  _resident((ng, gd, gd)),
                  pl.BlockSpec((1, d), lambda i, j: (0, 0))],
        out_specs=(pl.BlockSpec((1, tt, d), lambda i, j: (i, j, 0)),
                   pl.BlockSpec((1, POOL_HALO, d), lambda i, j: (i, 0, 0))),
        scratch_shapes=[pltpu.VMEM((tt + POOL_HALO, d), f32)],
        compiler_params=_params(("parallel", "arbitrary"), vmem),
        name="pool_layer",
    )(x, prev16, g.reshape(1, d), w_pool, scale.reshape(1, d))


def _s5_kernel(x_ref, g_ref, h0r_ref, h0i_ref, are_ref, aim_ref, bblk_ref, cblk_ref, d_ref,
               gout_ref, sr_ref, si_ref, u_ref, bu_ref, gel_ref, *, nb, tt):
    t = pl.program_id(0)
    npack = u_ref.shape[0]
    nsl = (S5_PACK * S5_STATE) // LANES

    @pl.when(t == 0)
    def _():
        sr_ref[...] = h0r_ref[...]
        si_ref[...] = h0i_ref[...]

    g = g_ref[...]
    for b in range(nb):
        u = _rms(x_ref[b], g)
        for p in range(npack):
            u_ref.at[p][pl.ds(b, tt, stride=nb), :] = u[:, p * LANES:(p + 1) * LANES]

    def pack_body(p, carry):
        u = u_ref[p]
        bu = jnp.dot(u.astype(bf16), bblk_ref[p], preferred_element_type=f32)
        for j in range(2 * nsl):
            bu_ref[j] = bu[:, j * LANES:(j + 1) * LANES]
        a_re = are_ref[p]
        a_im = aim_ref[p]
        ar = [jnp.broadcast_to(a_re[:, j * LANES:(j + 1) * LANES], (nb, LANES)) for j in range(nsl)]
        ai = [jnp.broadcast_to(a_im[:, j * LANES:(j + 1) * LANES], (nb, LANES)) for j in range(nsl)]
        s0r = sr_ref[p]
        s0i = si_ref[p]
        init = (tuple(s0r[:, j * LANES:(j + 1) * LANES] for j in range(nsl)),
                tuple(s0i[:, j * LANES:(j + 1) * LANES] for j in range(nsl)))

        def step(k, s):
            s_re, s_im = s
            new_re, new_im = [], []
            for j in range(nsl):
                rows = pl.ds(pl.multiple_of(k * nb, nb), nb)
                b_re = bu_ref.at[j][rows, :]
                b_im = bu_ref.at[nsl + j][rows, :]
                n_re = ar[j] * s_re[j] - ai[j] * s_im[j] + b_re
                n_im = ar[j] * s_im[j] + ai[j] * s_re[j] + b_im
                bu_ref.at[j][rows, :] = n_re
                bu_ref.at[nsl + j][rows, :] = n_im
                new_re.append(n_re)
                new_im.append(n_im)
            return tuple(new_re), tuple(new_im)

        s_re, s_im = lax.fori_loop(0, tt, step, init, unroll=SCAN_UNROLL)
        sr_ref[p] = jnp.concatenate(s_re, axis=-1)
        si_ref[p] = jnp.concatenate(s_im, axis=-1)
        states = jnp.concatenate([bu_ref[j] for j in range(2 * nsl)], axis=-1).astype(bf16)
        y = jnp.dot(states, cblk_ref[p], preferred_element_type=f32) + d_ref[p] * u
        gel_ref[p] = 0.5 * y * (1.0 + lax.erf(y * math.sqrt(0.5)))
        return carry

    lax.fori_loop(0, npack, pack_body, 0)
    for b in range(nb):
        for p in range(npack):
            gout_ref[b, :, p * LANES:(p + 1) * LANES] = gel_ref.at[p][pl.ds(b, tt, stride=nb), :].astype(bf16)


def _s5_scan(x, g, h0_re, h0_im, a_re, a_im, bblk, cblk, dvec, *, tt):
    b, t, d = x.shape
    npack = d // LANES
    sw = S5_PACK * S5_STATE
    rows = b * tt
    vmem = (2 * rows * d * 4 + rows * d * 4 + rows * 2 * sw * 4 + rows * d * 4 + 2 * rows * d * 2
            + 2 * npack * LANES * 2 * sw * 2 + 3 * rows * 2 * sw * 4)
    st_spec = pl.BlockSpec((npack, b, sw), lambda i: (0, 0, 0))
    return pl.pallas_call(
        functools.partial(_s5_kernel, nb=b, tt=tt),
        out_shape=(jax.ShapeDtypeStruct((b, t, d), bf16),
                   jax.ShapeDtypeStruct((npack, b, sw), f32),
                   jax.ShapeDtypeStruct((npack, b, sw), f32)),
        grid=(t // tt,),
        in_specs=[pl.BlockSpec((b, tt, d), lambda i: (0, i, 0)),
                  pl.BlockSpec((1, d), lambda i: (0, 0)),
                  st_spec, st_spec,
                  _resident((npack, 1, sw)), _resident((npack, 1, sw)),
                  _resident((npack, LANES, 2 * sw)), _resident((npack, 2 * sw, LANES)),
                  _resident((npack, 1, LANES))],
        out_specs=(pl.BlockSpec((b, tt, d), lambda i: (0, i, 0)), st_spec, st_spec),
        scratch_shapes=[pltpu.VMEM((npack, rows, LANES), f32),
                        pltpu.VMEM((2 * sw // LANES, rows, LANES), f32),
                        pltpu.VMEM((npack, rows, LANES), f32)],
        compiler_params=_params(("arbitrary",), vmem),
        name="s5_scan",
    )(x, g.reshape(1, d), h0_re, h0_im, a_re, a_im, bblk, cblk, dvec)


def _s5_discretise(a_re, a_im, b_re, b_im, c_re, c_im, log_dt):
    g, n = a_re.shape
    c = b_re.shape[-1]
    npack = g // S5_PACK
    dt = jnp.exp(log_dt)[:, None]
    mag = jnp.exp(dt * a_re)
    abar_re = mag * jnp.cos(dt * a_im)
    abar_im = mag * jnp.sin(dt * a_im)
    den = a_re * a_re + a_im * a_im
    nr = abar_re - 1.0
    f_re = (nr * a_re + abar_im * a_im) / den
    f_im = (abar_im * a_re - nr * a_im) / den
    bbar_re = f_re[..., None] * b_re - f_im[..., None] * b_im
    bbar_im = f_re[..., None] * b_im + f_im[..., None] * b_re
    eye = jnp.eye(S5_PACK, dtype=f32)

    def in_blocks(m):
        m = jnp.swapaxes(m, 1, 2).reshape(npack, S5_PACK, c, n)
        return jnp.einsum('pgcn,gh->pgchn', m, eye).reshape(npack, S5_PACK * c, S5_PACK * n)

    def out_blocks(m):
        m = jnp.swapaxes(m, 1, 2).reshape(npack, S5_PACK, n, c)
        return jnp.einsum('pgnc,gh->pgnhc', m, eye).reshape(npack, S5_PACK * n, S5_PACK * c)

    bblk = jnp.concatenate([in_blocks(bbar_re), in_blocks(bbar_im)], axis=2).astype(bf16)
    cblk = jnp.concatenate([out_blocks(c_re), -out_blocks(c_im)], axis=1).astype(bf16)
    pack = lambda v: v.reshape(npack, 1, S5_PACK * n)
    return pack(abar_re), pack(abar_im), bblk, cblk


def _pack_state(s):
    b, g, n = s.shape
    return jnp.swapaxes(s.reshape(b, g // S5_PACK, S5_PACK * n), 0, 1)


def _unpack_state(s):
    npack, b, sw = s.shape
    return jnp.swapaxes(s, 0, 1).reshape(b, npack * S5_PACK, S5_STATE)


def _qkv_kernel(x_ref, g_ref, w_ref, gq_ref, gk_ref, q_ref, k_ref, v_ref, h_ref, *, nblk):
    n = pl.program_id(1)

    @pl.when(n == 0)
    def _():
        h_ref[...] = _rms(x_ref[...], g_ref[...]).astype(bf16)

    r = jnp.dot(h_ref[...], w_ref[...], preferred_element_type=f32)

    def head_norm(gain):
        heads = [r[:, i * SB_HEAD_DIM:(i + 1) * SB_HEAD_DIM] for i in range(r.shape[1] // SB_HEAD_DIM)]
        return jnp.concatenate([_rms(c, gain) for c in heads], axis=-1)

    @pl.when(n < nblk)
    def _():
        q_ref[...] = head_norm(gq_ref[...]).astype(bf16)

    @pl.when((n >= nblk) & (n < 2 * nblk))
    def _():
        k_ref[...] = head_norm(gk_ref[...])

    @pl.when(n >= 2 * nblk)
    def _():
        v_ref[...] = r


def _qkv_proj(x, g, w_qkv, g_q, g_k, *, tm, tn):
    m, d = x.shape
    nblk = d // tn
    vmem = 2 * tm * d * 4 + tm * d * 2 + 2 * d * tn * 2 + 2 * 3 * tm * tn * 4 + 3 * tm * tn * 4

    def out_spec(which):
        return pl.BlockSpec((tm, tn), lambda i, n: (i, jnp.clip(n - which * nblk, 0, nblk - 1)))

    return pl.pallas_call(
        functools.partial(_qkv_kernel, nblk=nblk),
        out_shape=(jax.ShapeDtypeStruct((m, d), bf16), jax.ShapeDtypeStruct((m, d), f32),
                   jax.ShapeDtypeStruct((m, d), f32)),
        grid=(m // tm, 3 * nblk),
        in_specs=[pl.BlockSpec((tm, d), lambda i, n: (i, 0)),
                  pl.BlockSpec((1, d), lambda i, n: (0, 0)),
                  pl.BlockSpec((d, tn), lambda i, n: (0, n)),
                  pl.BlockSpec((1, SB_HEAD_DIM), lambda i, n: (0, 0)),
                  pl.BlockSpec((1, SB_HEAD_DIM), lambda i, n: (0, 0))],
        out_specs=(out_spec(0), out_spec(1), out_spec(2)),
        scratch_shapes=[pltpu.VMEM((tm, d), bf16)],
        compiler_params=_params(("parallel", "arbitrary"), vmem),
        name="qkv_proj",
    )(x, g.reshape(1, d), w_qkv, g_q.reshape(1, SB_HEAD_DIM), g_k.reshape(1, SB_HEAD_DIM))


def _later_matrix(tk):
    j = lax.broadcasted_iota(jnp.int32, (tk, tk), 0)
    s = lax.broadcasted_iota(jnp.int32, (tk, tk), 1)
    return jnp.where(j > s, 1.0, 0.0).astype(bf16)


def _sb_logw2(z2, later_m, mask):
    chunk = later_m.shape[0]
    l = jnp.log(1.0 + jnp.exp2(-jnp.abs(z2))) * LOG2E
    sp = jnp.maximum(z2, 0.0) + l
    if mask is not None:
        sp = jnp.where(mask, sp, 0.0)
    sp16 = sp.astype(bf16)
    nchunk = z2.shape[1] // chunk
    later, total = [None] * nchunk, None
    for c in range(nchunk - 1, -1, -1):
        cols = slice(c * chunk, (c + 1) * chunk)
        in_chunk = jnp.dot(sp16[:, cols], later_m, preferred_element_type=f32)
        later[c] = in_chunk if total is None else in_chunk + total
        rowsum = jnp.sum(sp[:, cols], axis=-1, keepdims=True)
        total = rowsum if total is None else total + rowsum
    later = later[0] if nchunk == 1 else jnp.concatenate(later, axis=1)
    return jnp.minimum(z2, 0.0) - l - later, total


def _sb_prompt_kernel(bias_ref, q_ref, k_ref, v_ref, o_ref, kb_ref, vb_ref, *, tq):
    kb_ref[...] = k_ref[0].astype(bf16)
    vb_ref[...] = v_ref[0].astype(bf16)
    bias2 = bias_ref[pl.program_id(1)] * LOG2E
    scale2 = LOG2E / math.sqrt(SB_HEAD_DIM)
    later_m = _later_matrix(min(SB_CHUNK, tq))
    r = lax.broadcasted_iota(jnp.int32, (tq, tq), 0)
    c = lax.broadcasted_iota(jnp.int32, (tq, tq), 1)
    causal = c < r
    for qi in range(q_ref.shape[1] // tq):
        q = q_ref[0, qi * tq:(qi + 1) * tq, :]
        acc = jnp.zeros((tq, SB_HEAD_DIM), f32)
        carry = None
        for kb in range(qi, -1, -1):
            rows = slice(kb * tq, (kb + 1) * tq)
            mask = causal if kb == qi else None
            z2 = lax.dot_general(q, kb_ref[rows, :], _NT, preferred_element_type=f32) * scale2 + bias2
            logw, tot = _sb_logw2(z2, later_m, mask)
            if carry is not None:
                logw = logw - carry
            w = jnp.exp2(logw)
            if mask is not None:
                w = jnp.where(mask, w, 0.0)
            acc = acc + jnp.dot(w.astype(bf16), vb_ref[rows, :], preferred_element_type=f32)
            carry = tot if carry is None else carry + tot
        o_ref[0, qi * tq:(qi + 1) * tq, :] = acc.astype(bf16)


def _sb_prompt(q, k, v, bias, *, tq):
    b, t, d = q.shape
    nh = d // SB_HEAD_DIM
    head = pl.BlockSpec((1, t, SB_HEAD_DIM), lambda i, h: (i, 0, h))
    vmem = 2 * 2 * t * SB_HEAD_DIM * 4 + 2 * 2 * t * SB_HEAD_DIM * 2 + 2 * t * SB_HEAD_DIM * 2 + 24 * tq * tq * 4
    return pl.pallas_call(
        functools.partial(_sb_prompt_kernel, tq=tq),
        out_shape=jax.ShapeDtypeStruct((b, t, d), bf16),
        grid=(b, nh),
        in_specs=[pl.BlockSpec(memory_space=pltpu.SMEM), head, head, head],
        out_specs=head,
        scratch_shapes=[pltpu.VMEM((t, SB_HEAD_DIM), bf16), pltpu.VMEM((t, SB_HEAD_DIM), bf16)],
        compiler_params=_params(("parallel", "parallel"), vmem),
        name="sb_prompt",
    )(bias, q, k, v)


def _sb_decode_kernel(pt_ref, q_ref, kn_ref, vn_ref, ck_ref, cv_ref, bias_ref, o_ref,
                      qbd_ref, qstage_ref, acc_ref, carry_ref, k2d_ref, v2d_ref, later_ref,
                      kbuf_ref, vbuf_ref, sem_ref, *, ts, nh, npg, layer, n_pages):
    b = pl.program_id(0)
    i = pl.program_id(1)
    nsteps = pl.num_programs(1) - 1
    nrow = nh * ts
    dh = SB_HEAD_DIM
    scale2 = LOG2E / math.sqrt(dh)

    def page_copies(step, buf):
        copies = []
        for s in range(npg):
            page = pt_ref[b, n_pages - step * npg + s]
            for h in range(nh):
                copies.append(pltpu.make_async_copy(ck_ref.at[layer, page, :, h, :], kbuf_ref.at[buf, s, h],
                                                    sem_ref.at[buf, 0]))
                copies.append(pltpu.make_async_copy(cv_ref.at[layer, page, :, h, :], vbuf_ref.at[buf, s, h],
                                                    sem_ref.at[buf, 1]))
        return copies

    def block(k2d, v2d, later_m, mask):
        z2 = (lax.dot_general(qbd_ref[...], k2d, _NT, preferred_element_type=f32) * scale2
              + bias_ref[...] * LOG2E)
        logw, tot = _sb_logw2(z2, later_m, mask)
        w = jnp.exp2(logw - carry_ref[...])
        if mask is not None:
            w = jnp.where(mask, w, 0.0)
        carry_ref[...] += tot
        out = jnp.dot(w.astype(bf16), v2d, preferred_element_type=f32)
        for h in range(nh):
            acc_ref[h * ts:(h + 1) * ts, :] += out[h * ts:(h + 1) * ts, h * dh:(h + 1) * dh]

    @pl.when(i == 0)
    def _():
        for c in page_copies(1, 1):
            c.start()
        qstage_ref[...] = jnp.zeros_like(qstage_ref)
        for h in range(nh):
            qstage_ref[h * ts:(h + 1) * ts, h * dh:(h + 1) * dh] = q_ref[0, :, h * dh:(h + 1) * dh]
        qbd_ref[...] = qstage_ref[...].astype(bf16)
        acc_ref[...] = jnp.zeros_like(acc_ref)
        carry_ref[...] = jnp.zeros_like(carry_ref)
        later_ref[...] = _later_matrix(later_ref.shape[0])
        nk = kn_ref.shape[1]
        t_of_row = lax.broadcasted_iota(jnp.int32, (nrow, nk), 0) % ts
        j = lax.broadcasted_iota(jnp.int32, (nrow, nk), 1)
        chunk = min(nk, later_ref.shape[0])
        block(kn_ref[0].astype(bf16), vn_ref[0].astype(bf16), later_ref[0:chunk, 0:chunk], j < t_of_row)

    def page_step(buf):
        @pl.when(i < nsteps)
        def _():
            for c in page_copies(i + 1, 1 - buf):
                c.start()
        for c in page_copies(i, buf):
            c.wait()
        for s in range(npg):
            for h in range(nh):
                keys = slice(s * PAGE_SIZE, (s + 1) * PAGE_SIZE)
                k2d_ref[keys, h * dh:(h + 1) * dh] = kbuf_ref[buf, s, h].astype(bf16)
                v2d_ref[keys, h * dh:(h + 1) * dh] = vbuf_ref[buf, s, h].astype(bf16)
        block(k2d_ref[...], v2d_ref[...], later_ref[...], None)

    @pl.when((i > 0) & (i % 2 == 1))
    def _():
        page_step(1)

    @pl.when((i > 0) & (i % 2 == 0))
    def _():
        page_step(0)

    @pl.when(i == nsteps)
    def _():
        for h in range(nh):
            o_ref[0, :, h * dh:(h + 1) * dh] = acc_ref[h * ts:(h + 1) * ts, :]


def _sb_decode(q, k_new, v_new, cache_k, cache_v, page_table, bias_rows, *, npg, layer):
    b, ts, d = q.shape
    nh = d // SB_HEAD_DIM
    n_pages = page_table.shape[1]
    nrow = nh * ts
    nkeys = npg * PAGE_SIZE
    chunk = min(SB_CHUNK, nkeys)
    per_b = lambda i, j, pt: (i, 0, 0)
    slabs = (2, npg, nh, PAGE_SIZE, SB_HEAD_DIM)
    vmem = (2 * 2 * npg * PAGE_SIZE * d * 4 + 2 * 2 * PAGE_SIZE * d * 4 + 2 * nkeys * d * 2
            + nrow * d * 6 + chunk * chunk * 2 + 4 * nrow * d * 4)
    return pl.pallas_call(
        functools.partial(_sb_decode_kernel, ts=ts, nh=nh, npg=npg, layer=layer, n_pages=n_pages),
        out_shape=jax.ShapeDtypeStruct((b, ts, d), f32),
        grid_spec=pltpu.PrefetchScalarGridSpec(
            num_scalar_prefetch=1,
            grid=(b, n_pages // npg + 1),
            in_specs=[pl.BlockSpec((1, ts, d), per_b),
                      pl.BlockSpec((1, PAGE_SIZE, d), per_b),
                      pl.BlockSpec((1, PAGE_SIZE, d), per_b),
                      pl.BlockSpec(memory_space=pl.ANY),
                      pl.BlockSpec(memory_space=pl.ANY),
                      pl.BlockSpec((nrow, 1), lambda i, j, pt: (0, 0))],
            out_specs=pl.BlockSpec((1, ts, d), per_b),
            scratch_shapes=[pltpu.VMEM((nrow, d), bf16), pltpu.VMEM((nrow, d), f32),
                            pltpu.VMEM((nrow, SB_HEAD_DIM), f32), pltpu.VMEM((nrow, 1), f32),
                            pltpu.VMEM((nkeys, d), bf16), pltpu.VMEM((nkeys, d), bf16),
                            pltpu.VMEM((chunk, chunk), bf16),
                            pltpu.VMEM(slabs, f32), pltpu.VMEM(slabs, f32),
                            pltpu.SemaphoreType.DMA((2, 2))]),
        compiler_params=_params(("arbitrary", "arbitrary"), vmem),
        name="sb_decode",
    )(page_table, q, k_new, v_new, cache_k, cache_v, bias_rows)


def kernel(x_prompt, x_sample, cache_pool, state_ssm_re, state_ssm_im, cache_k, cache_v, page_table,
           norm_mix, norm_ffn, w_ffn_gate, w_ffn_up, w_ffn_down, w_pool, pool_scale,
           ssm_a_re, ssm_a_im, ssm_b_re, ssm_b_im, ssm_c_re, ssm_c_im, ssm_d, ssm_log_dt,
           w_glu_a, w_glu_b, w_qkv, w_o, sb_q_norm, sb_k_norm, sb_bias):
    bp, tp, d = x_prompt.shape
    bs, ts, _ = x_sample.shape
    depth = norm_mix.shape[0]
    nh = d // SB_HEAD_DIM
    mp, ms = bp * tp, bs * ts
    past_len = page_table.shape[1] * PAGE_SIZE
    n_phys = cache_k.shape[1]
    tm = TM

    xp, xs = x_prompt, x_sample
    wg, wu, wd = w_ffn_gate.astype(bf16), w_ffn_up.astype(bf16), w_ffn_down.astype(bf16)
    pool_p, pool_s = [], []
    ssm_p, ssm_s = [], []
    kv_p, kv_s = [], []
    for i in range(depth):
        kind = i % N_MIXERS
        j = i // N_MIXERS
        if kind == 0:
            w = w_pool[j].astype(bf16)
            zero_prev = jnp.zeros((bp, POOL_HALO, d), f32)
            prev = jnp.pad(cache_pool[j], ((0, 0), (1, 0), (0, 0)))
            xp, buf_p = _pool_layer(xp, zero_prev, norm_mix[i], w, pool_scale[j], tt=POOL_TT, start_pos=0)
            xs, buf_s = _pool_layer(xs, prev, norm_mix[i], w, pool_scale[j], tt=ts, start_pos=past_len)
            pool_p.append(buf_p[:, 1:])
            pool_s.append(buf_s[:, 1:])
        elif kind == 1:
            a_re, a_im, bblk, cblk = _s5_discretise(ssm_a_re[j], ssm_a_im[j], ssm_b_re[j], ssm_b_im[j],
                                                    ssm_c_re[j], ssm_c_im[j], ssm_log_dt[j])
            dvec = ssm_d[j].reshape(d // LANES, 1, LANES)
            zero_state = jnp.zeros((d // LANES, bp, S5_PACK * S5_STATE), f32)
            gp, sr_p, si_p = _s5_scan(xp, norm_mix[i], zero_state, zero_state, a_re, a_im, bblk, cblk, dvec,
                                      tt=S5_TT)
            gs, sr_s, si_s = _s5_scan(xs, norm_mix[i], _pack_state(state_ssm_re[j]), _pack_state(state_ssm_im[j]),
                                      a_re, a_im, bblk, cblk, dvec, tt=ts)
            wa, wb = w_glu_a[j].astype(bf16), w_glu_b[j].astype(bf16)
            xp = _glu_residual(xp.reshape(mp, d), gp.reshape(mp, d), wa, wb, tm=TM_GLU).reshape(bp, tp, d)
            xs = _glu_residual(xs.reshape(ms, d), gs.reshape(ms, d), wa, wb, tm=ms).reshape(bs, ts, d)
            ssm_p.append((_unpack_state(sr_p), _unpack_state(si_p)))
            ssm_s.append((_unpack_state(sr_s), _unpack_state(si_s)))
        else:
            wqkv, wo = w_qkv[j].astype(bf16), w_o[j].astype(bf16)
            q_p, k_p, v_p = _qkv_proj(xp.reshape(mp, d), norm_mix[i], wqkv, sb_q_norm[j], sb_k_norm[j],
                                      tm=tm, tn=TN_QKV)
            q_s, k_s, v_s = _qkv_proj(xs.reshape(ms, d), norm_mix[i], wqkv, sb_q_norm[j], sb_k_norm[j],
                                      tm=ms, tn=TN_QKV)
            o_p = _sb_prompt(q_p.reshape(bp, tp, d), k_p.reshape(bp, tp, d), v_p.reshape(bp, tp, d), sb_bias[j],
                             tq=TQ)
            pad_new = lambda a: jnp.pad(a.reshape(bs, ts, d), ((0, 0), (0, PAGE_SIZE - ts), (0, 0)))
            o_s = _sb_decode(q_s.reshape(bs, ts, d).astype(f32), pad_new(k_s), pad_new(v_s),
                             cache_k, cache_v, page_table, jnp.repeat(sb_bias[j], ts).reshape(nh * ts, 1),
                             npg=DECODE_PAGES, layer=j)
            xp = _proj_residual(xp.reshape(mp, d), o_p.reshape(mp, d), wo, tm=tm).reshape(bp, tp, d)
            xs = _proj_residual(xs.reshape(ms, d), o_s.reshape(ms, d).astype(bf16), wo, tm=ms).reshape(bs, ts, d)
            kv_p.append((k_p.reshape(bp, tp, nh, SB_HEAD_DIM), v_p.reshape(bp, tp, nh, SB_HEAD_DIM)))
            kv_s.append((k_s.reshape(bs, ts, nh, SB_HEAD_DIM), v_s.reshape(bs, ts, nh, SB_HEAD_DIM)))
        xp = _ffn(xp.reshape(mp, d), norm_ffn[i], wg, wu, wd, i, tm=TM_FFN).reshape(bp, tp, d)
        xs = _ffn(xs.reshape(ms, d), norm_ffn[i], wg, wu, wd, i, tm=ms).reshape(bs, ts, d)

    stack = lambda items: jnp.stack(items, axis=0)
    return (xp, xs, stack(pool_p), stack(pool_s),
            stack([s[0] for s in ssm_p]), stack([s[1] for s in ssm_p]),
            stack([s[0] for s in ssm_s]), stack([s[1] for s in ssm_s]),
            stack([kv[0] for kv in kv_p]), stack([kv[1] for kv in kv_p]),
            stack([kv[0] for kv in kv_s]), stack([kv[1] for kv in kv_s]))
```

```python
import functools
import math

import jax
import jax.numpy as jnp
from jax import lax
from jax.experimental import pallas as pl
from jax.experimental.pallas import tpu as pltpu

f32 = jnp.float32
bf16 = jnp.bfloat16

RMS_EPS = 1e-6
N_MIXERS = 3
POOL_WINDOWS = (2, 4, 8, 16)
POOL_HALO = 16
S5_GROUP_SIZE = 16
S5_STATE = 64
S5_PACK = 8
SB_HEAD_DIM = 128
PAGE_SIZE = 128
LANES = 128
MIB = 1024 * 1024
VMEM_CAP = 56 * MIB

TM = 512
TM_FFN = 512
TN_QKV = 1024
TM_GLU = 256
POOL_TT = 512
S5_TT = 128
TQ = 512
SB_CHUNK = 256
SCAN_UNROLL = 8
DECODE_PAGES = 4
LOG2E = 1.4426950408889634

_NT = (((1,), (1,)), ((), ()))


def _params(sem, vmem_bytes):
    return pltpu.CompilerParams(dimension_semantics=sem,
                                vmem_limit_bytes=int(min(VMEM_CAP, vmem_bytes + 12 * MIB)))


def _rms(x, g):
    return x * lax.rsqrt(jnp.mean(x * x, axis=-1, keepdims=True) + RMS_EPS) * g


def _resident(shape):
    nd = len(shape)
    return pl.BlockSpec(shape, lambda *_: (0,) * nd, pipeline_mode=pl.Buffered(1))


def _ffn_kernel(x_ref, g_ref, wg0_ref, wu0_ref, wd0_ref, wg1_ref, wu1_ref, wd1_ref, o_ref, h_ref, *, nblk):
    f = pl.program_id(1)

    @pl.when(f == 0)
    def _():
        x = x_ref[...]
        h_ref[...] = _rms(x, g_ref[...]).astype(bf16)
        o_ref[...] = x

    h = h_ref[...]

    def f_block(wg_ref, wu_ref, wd_ref):
        a = jnp.dot(h, wg_ref[...], preferred_element_type=f32)
        b = jnp.dot(h, wu_ref[...], preferred_element_type=f32)
        act = (a * jax.nn.sigmoid(a) * b).astype(bf16)
        return jnp.dot(act, wd_ref[...], preferred_element_type=f32)

    @pl.when(2 * f + 1 < nblk)
    def _():
        o_ref[...] += f_block(wg0_ref, wu0_ref, wd0_ref) + f_block(wg1_ref, wu1_ref, wd1_ref)

    @pl.when(2 * f + 1 >= nblk)
    def _():
        o_ref[...] += f_block(wg0_ref, wu0_ref, wd0_ref)


def _ffn(x, g, wg, wu, wd, layer, *, tm, tf=512):
    m, d = x.shape
    nblk = wg.shape[2] // tf
    even = lambda j: 2 * j
    odd = lambda j: jnp.minimum(2 * j + 1, nblk - 1)
    cols = lambda blk: pl.BlockSpec((None, d, tf), lambda i, j: (layer, 0, blk(j)))
    rows = lambda blk: pl.BlockSpec((None, tf, d), lambda i, j: (layer, blk(j), 0))
    vmem = 2 * 2 * tm * d * 4 + tm * d * 2 + 2 * 6 * d * tf * 2 + 6 * tm * tf * 4
    return pl.pallas_call(
        functools.partial(_ffn_kernel, nblk=nblk),
        out_shape=jax.ShapeDtypeStruct((m, d), f32),
        grid=(m // tm, (nblk + 1) // 2),
        in_specs=[pl.BlockSpec((tm, d), lambda i, j: (i, 0)),
                  pl.BlockSpec((1, d), lambda i, j: (0, 0)),
                  cols(even), cols(even), rows(even), cols(odd), cols(odd), rows(odd)],
        out_specs=pl.BlockSpec((tm, d), lambda i, j: (i, 0)),
        scratch_shapes=[pltpu.VMEM((tm, d), bf16)],
        compiler_params=_params(("parallel", "arbitrary"), vmem),
        name="ffn",
    )(x, g.reshape(1, d), wg, wu, wd, wg, wu, wd)


def _proj_kernel(x_ref, a_ref, w_ref, o_ref):
    o_ref[...] = x_ref[...] + jnp.dot(a_ref[...], w_ref[...], preferred_element_type=f32)


def _proj_residual(x, a, w, *, tm):
    m, d = x.shape
    k = a.shape[1]
    vmem = 2 * 2 * tm * d * 4 + 2 * tm * k * 2 + k * d * 2 + tm * d * 4
    return pl.pallas_call(
        _proj_kernel,
        out_shape=jax.ShapeDtypeStruct((m, d), f32),
        grid=(m // tm,),
        in_specs=[pl.BlockSpec((tm, d), lambda i: (i, 0)),
                  pl.BlockSpec((tm, k), lambda i: (i, 0)),
                  _resident((k, d))],
        out_specs=pl.BlockSpec((tm, d), lambda i: (i, 0)),
        compiler_params=_params(("parallel",), vmem),
        name="proj_residual",
    )(x, a, w)


def _glu_kernel(x_ref, a_ref, wa_ref, wb_ref, o_ref):
    a = a_ref[...]
    p = jnp.dot(a, wa_ref[...], preferred_element_type=f32)
    q = jnp.dot(a, wb_ref[...], preferred_element_type=f32)
    o_ref[...] = x_ref[...] + p * jax.nn.sigmoid(q)


def _glu_residual(x, a, wa, wb, *, tm):
    m, d = x.shape
    k = a.shape[1]
    vmem = 2 * 2 * tm * d * 4 + 2 * tm * k * 2 + 2 * k * d * 2 + 3 * tm * d * 4
    return pl.pallas_call(
        _glu_kernel,
        out_shape=jax.ShapeDtypeStruct((m, d), f32),
        grid=(m // tm,),
        in_specs=[pl.BlockSpec((tm, d), lambda i: (i, 0)),
                  pl.BlockSpec((tm, k), lambda i: (i, 0)),
                  _resident((k, d)), _resident((k, d))],
        out_specs=pl.BlockSpec((tm, d), lambda i: (i, 0)),
        compiler_params=_params(("parallel",), vmem),
        name="glu_residual",
    )(x, a, wa, wb)


def _pool_kernel(x_ref, prev_ref, g_ref, w_ref, sc_ref, o_ref, buf_ref, ext_ref, *, tt, start_pos):
    t = pl.program_id(1)
    halo = POOL_HALO

    @pl.when(t == 0)
    def _():
        ext_ref[0:halo, :] = prev_ref[0]

    @pl.when(t > 0)
    def _():
        ext_ref[0:halo, :] = ext_ref[tt:tt + halo, :]

    x = x_ref[0]
    h = _rms(x, g_ref[...])
    ext_ref[halo:halo + tt, :] = h
    buf_ref[0] = ext_ref[tt:tt + halo, :]

    gd = x.shape[1] // len(POOL_WINDOWS)
    pos = start_pos + t * tt + lax.broadcasted_iota(jnp.int32, (tt, 1), 0)
    for gi, w in enumerate(POOL_WINDOWS):
        sl = slice(gi * gd, (gi + 1) * gd)
        win = h[:, sl]
        for i in range(1, w):
            win = win + ext_ref[halo - i:halo - i + tt, sl]
        cnt = jnp.minimum(pos + 1, w).astype(f32)
        p = win / cnt - h[:, sl]
        y = jnp.dot(p.astype(bf16), w_ref[gi], preferred_element_type=f32)
        o_ref[0, :, sl] = x[:, sl] + y * sc_ref[:, sl]


def _pool_layer(x, prev16, g, w_pool, scale, *, tt, start_pos):
    b, t, d = x.shape
    ng, gd = w_pool.shape[0], w_pool.shape[1]
    vmem = 2 * 2 * tt * d * 4 + (tt + POOL_HALO) * d * 4 + ng * gd * gd * 2 + 4 * tt * d * 4
    return pl.pallas_call(
        functools.partial(_pool_kernel, tt=tt, start_pos=start_pos),
        out_shape=(jax.ShapeDtypeStruct((b, t, d), f32), jax.ShapeDtypeStruct((b, POOL_HALO, d), f32)),
        grid=(b, t // tt),
        in_specs=[pl.BlockSpec((1, tt, d), lambda i, j: (i, j, 0)),
                  pl.BlockSpec((1, POOL_HALO, d), lambda i, j: (i, 0, 0)),
                  pl.BlockSpec((1, d), lambda i, j: (0, 0)),
                  _resident((ng, gd, gd)),
                  pl.BlockSpec((1, d), lambda i, j: (0, 0))],
        out_specs=(pl.BlockSpec((1, tt, d), lambda i, j: (i, j, 0)),
                   pl.BlockSpec((1, POOL_HALO, d), lambda i, j: (i, 0, 0))),
        scratch_shapes=[pltpu.VMEM((tt + POOL_HALO, d), f32)],
        compiler_params=_params(("parallel", "arbitrary"), vmem),
        name="pool_layer",
    )(x, prev16, g.reshape(1, d), w_pool, scale.reshape(1, d))


def _s5_kernel(x_ref, g_ref, h0r_ref, h0i_ref, are_ref, aim_ref, bblk_ref, cblk_ref, d_ref,
               gout_ref, sr_ref, si_ref, u_ref, bu_ref, gel_ref, *, nb, tt):
    t = pl.program_id(0)
    npack = u_ref.shape[0]
    nsl = (S5_PACK * S5_STATE) // LANES

    @pl.when(t == 0)
    def _():
        sr_ref[...] = h0r_ref[...]
        si_ref[...] = h0i_ref[...]

    g = g_ref[...]
    for b in range(nb):
        u = _rms(x_ref[b], g)
        for p in range(npack):
            u_ref.at[p][pl.ds(b, tt, stride=nb), :] = u[:, p * LANES:(p + 1) * LANES]

    def pack_body(p, carry):
        u = u_ref[p]
        bu = jnp.dot(u.astype(bf16), bblk_ref[p], preferred_element_type=f32)
        for j in range(2 * nsl):
            bu_ref[j] = bu[:, j * LANES:(j + 1) * LANES]
        a_re = are_ref[p]
        a_im = aim_ref[p]
        ar = [jnp.broadcast_to(a_re[:, j * LANES:(j + 1) * LANES], (nb, LANES)) for j in range(nsl)]
        ai = [jnp.broadcast_to(a_im[:, j * LANES:(j + 1) * LANES], (nb, LANES)) for j in range(nsl)]
        s0r = sr_ref[p]
        s0i = si_ref[p]
        init = (tuple(s0r[:, j * LANES:(j + 1) * LANES] for j in range(nsl)),
                tuple(s0i[:, j * LANES:(j + 1) * LANES] for j in range(nsl)))

        def step(k, s):
            s_re, s_im = s
            new_re, new_im = [], []
            for j in range(nsl):
                rows = pl.ds(pl.multiple_of(k * nb, nb), nb)
                b_re = bu_ref.at[j][rows, :]
                b_im = bu_ref.at[nsl + j][rows, :]
                n_re = ar[j] * s_re[j] - ai[j] * s_im[j] + b_re
                n_im = ar[j] * s_im[j] + ai[j] * s_re[j] + b_im
                bu_ref.at[j][rows, :] = n_re
                bu_ref.at[nsl + j][rows, :] = n_im
                new_re.append(n_re)
                new_im.append(n_im)
            return tuple(new_re), tuple(new_im)

        s_re, s_im = lax.fori_loop(0, tt, step, init, unroll=SCAN_UNROLL)
        sr_ref[p] = jnp.concatenate(s_re, axis=-1)
        si_ref[p] = jnp.concatenate(s_im, axis=-1)
        states = jnp.concatenate([bu_ref[j] for j in range(2 * nsl)], axis=-1).astype(bf16)
        y = jnp.dot(states, cblk_ref[p], preferred_element_type=f32) + d_ref[p] * u
        gel_ref[p] = 0.5 * y * (1.0 + lax.erf(y * math.sqrt(0.5)))
        return carry

    lax.fori_loop(0, npack, pack_body, 0)
    for b in range(nb):
        for p in range(npack):
            gout_ref[b, :, p * LANES:(p + 1) * LANES] = gel_ref.at[p][pl.ds(b, tt, stride=nb), :].astype(bf16)


def _s5_scan(x, g, h0_re, h0_im, a_re, a_im, bblk, cblk, dvec, *, tt):
    b, t, d = x.shape
    npack = d // LANES
    sw = S5_PACK * S5_STATE
    rows = b * tt
    vmem = (2 * rows * d * 4 + rows * d * 4 + rows * 2 * sw * 4 + rows * d * 4 + 2 * rows * d * 2
            + 2 * npack * LANES * 2 * sw * 2 + 3 * rows * 2 * sw * 4)
    st_spec = pl.BlockSpec((npack, b, sw), lambda i: (0, 0, 0))
    return pl.pallas_call(
        functools.partial(_s5_kernel, nb=b, tt=tt),
        out_shape=(jax.ShapeDtypeStruct((b, t, d), bf16),
                   jax.ShapeDtypeStruct((npack, b, sw), f32),
                   jax.ShapeDtypeStruct((npack, b, sw), f32)),
        grid=(t // tt,),
        in_specs=[pl.BlockSpec((b, tt, d), lambda i: (0, i, 0)),
                  pl.BlockSpec((1, d), lambda i: (0, 0)),
                  st_spec, st_spec,
                  _resident((npack, 1, sw)), _resident((npack, 1, sw)),
                  _resident((npack, LANES, 2 * sw)), _resident((npack, 2 * sw, LANES)),
                  _resident((npack, 1, LANES))],
        out_specs=(pl.BlockSpec((b, tt, d), lambda i: (0, i, 0)), st_spec, st_spec),
        scratch_shapes=[pltpu.VMEM((npack, rows, LANES), f32),
                        pltpu.VMEM((2 * sw // LANES, rows, LANES), f32),
                        pltpu.VMEM((npack, rows, LANES), f32)],
        compiler_params=_params(("arbitrary",), vmem),
        name="s5_scan",
    )(x, g.reshape(1, d), h0_re, h0_im, a_re, a_im, bblk, cblk, dvec)


def _s5_discretise(a_re, a_im, b_re, b_im, c_re, c_im, log_dt):
    g, n = a_re.shape
    c = b_re.shape[-1]
    npack = g // S5_PACK
    dt = jnp.exp(log_dt)[:, None]
    mag = jnp.exp(dt * a_re)
    abar_re = mag * jnp.cos(dt * a_im)
    abar_im = mag * jnp.sin(dt * a_im)
    den = a_re * a_re + a_im * a_im
    nr = abar_re - 1.0
    f_re = (nr * a_re + abar_im * a_im) / den
    f_im = (abar_im * a_re - nr * a_im) / den
    bbar_re = f_re[..., None] * b_re - f_im[..., None] * b_im
    bbar_im = f_re[..., None] * b_im + f_im[..., None] * b_re
    eye = jnp.eye(S5_PACK, dtype=f32)

    def in_blocks(m):
        m = jnp.swapaxes(m, 1, 2).reshape(npack, S5_PACK, c, n)
        return jnp.einsum('pgcn,gh->pgchn', m, eye).reshape(npack, S5_PACK * c, S5_PACK * n)

    def out_blocks(m):
        m = jnp.swapaxes(m, 1, 2).reshape(npack, S5_PACK, n, c)
        return jnp.einsum('pgnc,gh->pgnhc', m, eye).reshape(npack, S5_PACK * n, S5_PACK * c)

    bblk = jnp.concatenate([in_blocks(bbar_re), in_blocks(bbar_im)], axis=2).astype(bf16)
    cblk = jnp.concatenate([out_blocks(c_re), -out_blocks(c_im)], axis=1).astype(bf16)
    pack = lambda v: v.reshape(npack, 1, S5_PACK * n)
    return pack(abar_re), pack(abar_im), bblk, cblk


def _pack_state(s):
    b, g, n = s.shape
    return jnp.swapaxes(s.reshape(b, g // S5_PACK, S5_PACK * n), 0, 1)


def _unpack_state(s):
    npack, b, sw = s.shape
    return jnp.swapaxes(s, 0, 1).reshape(b, npack * S5_PACK, S5_STATE)


def _qkv_kernel(x_ref, g_ref, w_ref, gq_ref, gk_ref, q_ref, k_ref, v_ref, h_ref, *, nblk):
    n = pl.program_id(1)

    @pl.when(n == 0)
    def _():
        h_ref[...] = _rms(x_ref[...], g_ref[...]).astype(bf16)

    r = jnp.dot(h_ref[...], w_ref[...], preferred_element_type=f32)

    def head_norm(gain):
        heads = [r[:, i * SB_HEAD_DIM:(i + 1) * SB_HEAD_DIM] for i in range(r.shape[1] // SB_HEAD_DIM)]
        return jnp.concatenate([_rms(c, gain) for c in heads], axis=-1)

    @pl.when(n < nblk)
    def _():
        q_ref[...] = head_norm(gq_ref[...]).astype(bf16)

    @pl.when((n >= nblk) & (n < 2 * nblk))
    def _():
        k_ref[...] = head_norm(gk_ref[...])

    @pl.when(n >= 2 * nblk)
    def _():
        v_ref[...] = r


def _qkv_proj(x, g, w_qkv, g_q, g_k, *, tm, tn):
    m, d = x.shape
    nblk = d // tn
    vmem = 2 * tm * d * 4 + tm * d * 2 + 2 * d * tn * 2 + 2 * 3 * tm * tn * 4 + 3 * tm * tn * 4

    def out_spec(which):
        return pl.BlockSpec((tm, tn), lambda i, n: (i, jnp.clip(n - which * nblk, 0, nblk - 1)))

    return pl.pallas_call(
        functools.partial(_qkv_kernel, nblk=nblk),
        out_shape=(jax.ShapeDtypeStruct((m, d), bf16), jax.ShapeDtypeStruct((m, d), f32),
                   jax.ShapeDtypeStruct((m, d), f32)),
        grid=(m // tm, 3 * nblk),
        in_specs=[pl.BlockSpec((tm, d), lambda i, n: (i, 0)),
                  pl.BlockSpec((1, d), lambda i, n: (0, 0)),
                  pl.BlockSpec((d, tn), lambda i, n: (0, n)),
                  pl.BlockSpec((1, SB_HEAD_DIM), lambda i, n: (0, 0)),
                  pl.BlockSpec((1, SB_HEAD_DIM), lambda i, n: (0, 0))],
        out_specs=(out_spec(0), out_spec(1), out_spec(2)),
        scratch_shapes=[pltpu.VMEM((tm, d), bf16)],
        compiler_params=_params(("parallel", "arbitrary"), vmem),
        name="qkv_proj",
    )(x, g.reshape(1, d), w_qkv, g_q.reshape(1, SB_HEAD_DIM), g_k.reshape(1, SB_HEAD_DIM))


def _later_matrix(tk):
    j = lax.broadcasted_iota(jnp.int32, (tk, tk), 0)
    s = lax.broadcasted_iota(jnp.int32, (tk, tk), 1)
    return jnp.where(j > s, 1.0, 0.0).astype(bf16)


def _sb_logw2(z2, later_m, mask):
    chunk = later_m.shape[0]
    l = jnp.log(1.0 + jnp.exp2(-jnp.abs(z2))) * LOG2E
    sp = jnp.maximum(z2, 0.0) + l
    if mask is not None:
        sp = jnp.where(mask, sp, 0.0)
    sp16 = sp.astype(bf16)
    nchunk = z2.shape[1] // chunk
    later, total = [None] * nchunk, None
    for c in range(nchunk - 1, -1, -1):
        cols = slice(c * chunk, (c + 1) * chunk)
        in_chunk = jnp.dot(sp16[:, cols], later_m, preferred_element_type=f32)
        later[c] = in_chunk if total is None else in_chunk + total
        rowsum = jnp.sum(sp[:, cols], axis=-1, keepdims=True)
        total = rowsum if total is None else total + rowsum
    later = later[0] if nchunk == 1 else jnp.concatenate(later, axis=1)
    return jnp.minimum(z2, 0.0) - l - later, total


def _sb_prompt_kernel(bias_ref, q_ref, k_ref, v_ref, o_ref, kb_ref, vb_ref, *, tq):
    kb_ref[...] = k_ref[0].astype(bf16)
    vb_ref[...] = v_ref[0].astype(bf16)
    bias2 = bias_ref[pl.program_id(1)] * LOG2E
    scale2 = LOG2E / math.sqrt(SB_HEAD_DIM)
    later_m = _later_matrix(min(SB_CHUNK, tq))
    r = lax.broadcasted_iota(jnp.int32, (tq, tq), 0)
    c = lax.broadcasted_iota(jnp.int32, (tq, tq), 1)
    causal = c < r
    for qi in range(q_ref.shape[1] // tq):
        q = q_ref[0, qi * tq:(qi + 1) * tq, :]
        acc = jnp.zeros((tq, SB_HEAD_DIM), f32)
        carry = None
        for kb in range(qi, -1, -1):
            rows = slice(kb * tq, (kb + 1) * tq)
            mask = causal if kb == qi else None
            z2 = lax.dot_general(q, kb_ref[rows, :], _NT, preferred_element_type=f32) * scale2 + bias2
            logw, tot = _sb_logw2(z2, later_m, mask)
            if carry is not None:
                logw = logw - carry
            w = jnp.exp2(logw)
            if mask is not None:
                w = jnp.where(mask, w, 0.0)
            acc = acc + jnp.dot(w.astype(bf16), vb_ref[rows, :], preferred_element_type=f32)
            carry = tot if carry is None else carry + tot
        o_ref[0, qi * tq:(qi + 1) * tq, :] = acc.astype(bf16)


def _sb_prompt(q, k, v, bias, *, tq):
    b, t, d = q.shape
    nh = d // SB_HEAD_DIM
    head = pl.BlockSpec((1, t, SB_HEAD_DIM), lambda i, h: (i, 0, h))
    vmem = 2 * 2 * t * SB_HEAD_DIM * 4 + 2 * 2 * t * SB_HEAD_DIM * 2 + 2 * t * SB_HEAD_DIM * 2 + 24 * tq * tq * 4
    return pl.pallas_call(
        functools.partial(_sb_prompt_kernel, tq=tq),
        out_shape=jax.ShapeDtypeStruct((b, t, d), bf16),
        grid=(b, nh),
        in_specs=[pl.BlockSpec(memory_space=pltpu.SMEM), head, head, head],
        out_specs=head,
        scratch_shapes=[pltpu.VMEM((t, SB_HEAD_DIM), bf16), pltpu.VMEM((t, SB_HEAD_DIM), bf16)],
        compiler_params=_params(("parallel", "parallel"), vmem),
        name="sb_prompt",
    )(bias, q, k, v)


def _sb_decode_kernel(pt_ref, q_ref, kn_ref, vn_ref, ck_ref, cv_ref, bias_ref, o_ref,
                      qbd_ref, qstage_ref, acc_ref, carry_ref, k2d_ref, v2d_ref, later_ref,
                      kbuf_ref, vbuf_ref, sem_ref, *, ts, nh, npg, layer, n_pages):
    b = pl.program_id(0)
    i = pl.program_id(1)
    nsteps = pl.num_programs(1) - 1
    nrow = nh * ts
    dh = SB_HEAD_DIM
    scale2 = LOG2E / math.sqrt(dh)

    def page_copies(step, buf):
        copies = []
        for s in range(npg):
            page = pt_ref[b, n_pages - step * npg + s]
            for h in range(nh):
                copies.append(pltpu.make_async_copy(ck_ref.at[layer, page, :, h, :], kbuf_ref.at[buf, s, h],
                                                    sem_ref.at[buf, 0]))
                copies.append(pltpu.make_async_copy(cv_ref.at[layer, page, :, h, :], vbuf_ref.at[buf, s, h],
                                                    sem_ref.at[buf, 1]))
        return copies

    def block(k2d, v2d, later_m, mask):
        z2 = (lax.dot_general(qbd_ref[...], k2d, _NT, preferred_element_type=f32) * scale2
              + bias_ref[...] * LOG2E)
        logw, tot = _sb_logw2(z2, later_m, mask)
        w = jnp.exp2(logw - carry_ref[...])
        if mask is not None:
            w = jnp.where(mask, w, 0.0)
        carry_ref[...] += tot
        out = jnp.dot(w.astype(bf16), v2d, preferred_element_type=f32)
        for h in range(nh):
            acc_ref[h * ts:(h + 1) * ts, :] += out[h * ts:(h + 1) * ts, h * dh:(h + 1) * dh]

    @pl.when(i == 0)
    def _():
        for c in page_copies(1, 1):
            c.start()
        qstage_ref[...] = jnp.zeros_like(qstage_ref)
        for h in range(nh):
            qstage_ref[h * ts:(h + 1) * ts, h * dh:(h + 1) * dh] = q_ref[0, :, h * dh:(h + 1) * dh]
        qbd_ref[...] = qstage_ref[...].astype(bf16)
        acc_ref[...] = jnp.zeros_like(acc_ref)
        carry_ref[...] = jnp.zeros_like(carry_ref)
        later_ref[...] = _later_matrix(later_ref.shape[0])
        nk = kn_ref.shape[1]
        t_of_row = lax.broadcasted_iota(jnp.int32, (nrow, nk), 0) % ts
        j = lax.broadcasted_iota(jnp.int32, (nrow, nk), 1)
        chunk = min(nk, later_ref.shape[0])
        block(kn_ref[0].astype(bf16), vn_ref[0].astype(bf16), later_ref[0:chunk, 0:chunk], j < t_of_row)

    def page_step(buf):
        @pl.when(i < nsteps)
        def _():
            for c in page_copies(i + 1, 1 - buf):
                c.start()
        for c in page_copies(i, buf):
            c.wait()
        for s in range(npg):
            for h in range(nh):
                keys = slice(s * PAGE_SIZE, (s + 1) * PAGE_SIZE)
                k2d_ref[keys, h * dh:(h + 1) * dh] = kbuf_ref[buf, s, h].astype(bf16)
                v2d_ref[keys, h * dh:(h + 1) * dh] = vbuf_ref[buf, s, h].astype(bf16)
        block(k2d_ref[...], v2d_ref[...], later_ref[...], None)

    @pl.when((i > 0) & (i % 2 == 1))
    def _():
        page_step(1)

    @pl.when((i > 0) & (i % 2 == 0))
    def _():
        page_step(0)

    @pl.when(i == nsteps)
    def _():
        for h in range(nh):
            o_ref[0, :, h * dh:(h + 1) * dh] = acc_ref[h * ts:(h + 1) * ts, :]


def _sb_decode(q, k_new, v_new, cache_k, cache_v, page_table, bias_rows, *, npg, layer):
    b, ts, d = q.shape
    nh = d // SB_HEAD_DIM
    n_pages = page_table.shape[1]
    nrow = nh * ts
    nkeys = npg * PAGE_SIZE
    chunk = min(SB_CHUNK, nkeys)
    per_b = lambda i, j, pt: (i, 0, 0)
    slabs = (2, npg, nh, PAGE_SIZE, SB_HEAD_DIM)
    vmem = (2 * 2 * npg * PAGE_SIZE * d * 4 + 2 * 2 * PAGE_SIZE * d * 4 + 2 * nkeys * d * 2
            + nrow * d * 6 + chunk * chunk * 2 + 4 * nrow * d * 4)
    return pl.pallas_call(
        functools.partial(_sb_decode_kernel, ts=ts, nh=nh, npg=npg, layer=layer, n_pages=n_pages),
        out_shape=jax.ShapeDtypeStruct((b, ts, d), f32),
        grid_spec=pltpu.PrefetchScalarGridSpec(
            num_scalar_prefetch=1,
            grid=(b, n_pages // npg + 1),
            in_specs=[pl.BlockSpec((1, ts, d), per_b),
                      pl.BlockSpec((1, PAGE_SIZE, d), per_b),
                      pl.BlockSpec((1, PAGE_SIZE, d), per_b),
                      pl.BlockSpec(memory_space=pl.ANY),
                      pl.BlockSpec(memory_space=pl.ANY),
                      pl.BlockSpec((nrow, 1), lambda i, j, pt: (0, 0))],
            out_specs=pl.BlockSpec((1, ts, d), per_b),
            scratch_shapes=[pltpu.VMEM((nrow, d), bf16), pltpu.VMEM((nrow, d), f32),
                            pltpu.VMEM((nrow, SB_HEAD_DIM), f32), pltpu.VMEM((nrow, 1), f32),
                            pltpu.VMEM((nkeys, d), bf16), pltpu.VMEM((nkeys, d), bf16),
                            pltpu.VMEM((chunk, chunk), bf16),
                            pltpu.VMEM(slabs, f32), pltpu.VMEM(slabs, f32),
                            pltpu.SemaphoreType.DMA((2, 2))]),
        compiler_params=_params(("arbitrary", "arbitrary"), vmem),
        name="sb_decode",
    )(page_table, q, k_new, v_new, cache_k, cache_v, bias_rows)


def kernel(x_prompt, x_sample, cache_pool, state_ssm_re, state_ssm_im, cache_k, cache_v, page_table,
           norm_mix, norm_ffn, w_ffn_gate, w_ffn_up, w_ffn_down, w_pool, pool_scale,
           ssm_a_re, ssm_a_im, ssm_b_re, ssm_b_im, ssm_c_re, ssm_c_im, ssm_d, ssm_log_dt,
           w_glu_a, w_glu_b, w_qkv, w_o, sb_q_norm, sb_k_norm, sb_bias):
    bp, tp, d = x_prompt.shape
    bs, ts, _ = x_sample.shape
    depth = norm_mix.shape[0]
    nh = d // SB_HEAD_DIM
    mp, ms = bp * tp, bs * ts
    past_len = page_table.shape[1] * PAGE_SIZE
    n_phys = cache_k.shape[1]
    tm = TM

    xp, xs = x_prompt, x_sample
    wg, wu, wd = w_ffn_gate.astype(bf16), w_ffn_up.astype(bf16), w_ffn_down.astype(bf16)
    pool_p, pool_s = [], []
    ssm_p, ssm_s = [], []
    kv_p, kv_s = [], []
    for i in range(depth):
        kind = i % N_MIXERS
        j = i // N_MIXERS
        if kind == 0:
            w = w_pool[j].astype(bf16)
            zero_prev = jnp.zeros((bp, POOL_HALO, d), f32)
            prev = jnp.pad(cache_pool[j], ((0, 0), (1, 0), (0, 0)))
            xp, buf_p = _pool_layer(xp, zero_prev, norm_mix[i], w, pool_scale[j], tt=POOL_TT, start_pos=0)
            xs, buf_s = _pool_layer(xs, prev, norm_mix[i], w, pool_scale[j], tt=ts, start_pos=past_len)
            pool_p.append(buf_p[:, 1:])
            pool_s.append(buf_s[:, 1:])
        elif kind == 1:
            a_re, a_im, bblk, cblk = _s5_discretise(ssm_a_re[j], ssm_a_im[j], ssm_b_re[j], ssm_b_im[j],
                                                    ssm_c_re[j], ssm_c_im[j], ssm_log_dt[j])
            dvec = ssm_d[j].reshape(d // LANES, 1, LANES)
            zero_state = jnp.zeros((d // LANES, bp, S5_PACK * S5_STATE), f32)
            gp, sr_p, si_p = _s5_scan(xp, norm_mix[i], zero_state, zero_state, a_re, a_im, bblk, cblk, dvec,
                                      tt=S5_TT)
            gs, sr_s, si_s = _s5_scan(xs, norm_mix[i], _pack_state(state_ssm_re[j]), _pack_state(state_ssm_im[j]),
                                      a_re, a_im, bblk, cblk, dvec, tt=ts)
            wa, wb = w_glu_a[j].astype(bf16), w_glu_b[j].astype(bf16)
            xp = _glu_residual(xp.reshape(mp, d), gp.reshape(mp, d), wa, wb, tm=TM_GLU).reshape(bp, tp, d)
            xs = _glu_residual(xs.reshape(ms, d), gs.reshape(ms, d), wa, wb, tm=ms).reshape(bs, ts, d)
            ssm_p.append((_unpack_state(sr_p), _unpack_state(si_p)))
            ssm_s.append((_unpack_state(sr_s), _unpack_state(si_s)))
        else:
            wqkv, wo = w_qkv[j].astype(bf16), w_o[j].astype(bf16)
            q_p, k_p, v_p = _qkv_proj(xp.reshape(mp, d), norm_mix[i], wqkv, sb_q_norm[j], sb_k_norm[j],
                                      tm=tm, tn=TN_QKV)
            q_s, k_s, v_s = _qkv_proj(xs.reshape(ms, d), norm_mix[i], wqkv, sb_q_norm[j], sb_k_norm[j],
                                      tm=ms, tn=TN_QKV)
            o_p = _sb_prompt(q_p.reshape(bp, tp, d), k_p.reshape(bp, tp, d), v_p.reshape(bp, tp, d), sb_bias[j],
                             tq=TQ)
            pad_new = lambda a: jnp.pad(a.reshape(bs, ts, d), ((0, 0), (0, PAGE_SIZE - ts), (0, 0)))
            o_s = _sb_decode(q_s.reshape(bs, ts, d).astype(f32), pad_new(k_s), pad_new(v_s),
                             cache_k, cache_v, page_table, jnp.repeat(sb_bias[j], ts).reshape(nh * ts, 1),
                             npg=DECODE_PAGES, layer=j)
            xp = _proj_residual(xp.reshape(mp, d), o_p.reshape(mp, d), wo, tm=tm).reshape(bp, tp, d)
            xs = _proj_residual(xs.reshape(ms, d), o_s.reshape(ms, d).astype(bf16), wo, tm=ms).reshape(bs, ts, d)
            kv_p.append((k_p.reshape(bp, tp, nh, SB_HEAD_DIM), v_p.reshape(bp, tp, nh, SB_HEAD_DIM)))
            kv_s.append((k_s.reshape(bs, ts, nh, SB_HEAD_DIM), v_s.reshape(bs, ts, nh, SB_HEAD_DIM)))
        xp = _ffn(xp.reshape(mp, d), norm_ffn[i], wg, wu, wd, i, tm=TM_FFN).reshape(bp, tp, d)
        xs = _ffn(xs.reshape(ms, d), norm_ffn[i], wg, wu, wd, i, tm=ms).reshape(bs, ts, d)

    stack = lambda items: jnp.stack(items, axis=0)
    return (xp, xs, stack(pool_p), stack(pool_s),
            stack([s[0] for s in ssm_p]), stack([s[1] for s in ssm_p]),
            stack([s[0] for s in ssm_s]), stack([s[1] for s in ssm_s]),
            stack([kv[0] for kv in kv_p]), stack([kv[1] for kv in kv_p]),
            stack([kv[0] for kv in kv_s]), stack([kv[1] for kv in kv_s]))
```

```python
import functools
import math

import jax
import jax.numpy as jnp
from jax import lax
from jax.experimental import pallas as pl
from jax.experimental.pallas import tpu as pltpu

f32 = jnp.float32
bf16 = jnp.bfloat16

RMS_EPS = 1e-6
N_MIXERS = 3
POOL_WINDOWS = (2, 4, 8, 16)
POOL_HALO = 16
POOL_PAD = 8
S5_GROUP_SIZE = 16
S5_STATE = 64
S5_PACK = 8
SB_HEAD_DIM = 128
PAGE_SIZE = 128
LANES = 128
MIB = 1024 * 1024
VMEM_CAP = 56 * MIB

TM = 512
TM_FFN = 512
TN_QKV = 1024
TM_GLU = 256
POOL_TT = 512
S5_TT = 128
TQ = 512
SB_CHUNK = 256
SCAN_UNROLL = 8
DECODE_PAGES = 4
LOG2E = 1.4426950408889634

_NT = (((1,), (1,)), ((), ()))


def _params(sem, vmem_bytes):
    return pltpu.CompilerParams(dimension_semantics=sem,
                                vmem_limit_bytes=int(min(VMEM_CAP, vmem_bytes + 12 * MIB)))


def _rms(x, g):
    return x * lax.rsqrt(jnp.mean(x * x, axis=-1, keepdims=True) + RMS_EPS) * g


def _resident(shape):
    nd = len(shape)
    return pl.BlockSpec(shape, lambda *_: (0,) * nd, pipeline_mode=pl.Buffered(1))


def _ffn_kernel(x_ref, g_ref, wg0_ref, wu0_ref, wd0_ref, wg1_ref, wu1_ref, wd1_ref, o_ref, h_ref, *, nblk):
    f = pl.program_id(1)

    @pl.when(f == 0)
    def _():
        x = x_ref[...]
        h_ref[...] = _rms(x, g_ref[...]).astype(bf16)
        o_ref[...] = x

    h = h_ref[...]

    def f_block(wg_ref, wu_ref, wd_ref):
        a = jnp.dot(h, wg_ref[...], preferred_element_type=f32)
        b = jnp.dot(h, wu_ref[...], preferred_element_type=f32)
        act = (a * jax.nn.sigmoid(a) * b).astype(bf16)
        return jnp.dot(act, wd_ref[...], preferred_element_type=f32)

    @pl.when(2 * f + 1 < nblk)
    def _():
        o_ref[...] += f_block(wg0_ref, wu0_ref, wd0_ref) + f_block(wg1_ref, wu1_ref, wd1_ref)

    @pl.when(2 * f + 1 >= nblk)
    def _():
        o_ref[...] += f_block(wg0_ref, wu0_ref, wd0_ref)


def _ffn(x, g, wg, wu, wd, layer, *, tm, tf=512):
    m, d = x.shape
    nblk = wg.shape[2] // tf
    even = lambda j: 2 * j
    odd = lambda j: jnp.where(2 * j + 1 < nblk, 2 * j + 1, 2 * j - 1)
    cols = lambda blk: pl.BlockSpec((None, d, tf), lambda i, j: (layer, 0, blk(j)))
    rows = lambda blk: pl.BlockSpec((None, tf, d), lambda i, j: (layer, blk(j), 0))
    vmem = 2 * 2 * tm * d * 4 + tm * d * 2 + 2 * 6 * d * tf * 2 + 6 * tm * tf * 4
    return pl.pallas_call(
        functools.partial(_ffn_kernel, nblk=nblk),
        out_shape=jax.ShapeDtypeStruct((m, d), f32),
        grid=(m // tm, (nblk + 1) // 2),
        in_specs=[pl.BlockSpec((tm, d), lambda i, j: (i, 0)),
                  pl.BlockSpec((1, d), lambda i, j: (0, 0)),
                  cols(even), cols(even), rows(even), cols(odd), cols(odd), rows(odd)],
        out_specs=pl.BlockSpec((tm, d), lambda i, j: (i, 0)),
        scratch_shapes=[pltpu.VMEM((tm, d), bf16)],
        compiler_params=_params(("parallel", "arbitrary"), vmem),
        name="ffn",
    )(x, g.reshape(1, d), wg, wu, wd, wg, wu, wd)


def _proj_kernel(x_ref, a_ref, w_ref, o_ref):
    o_ref[...] = x_ref[...] + jnp.dot(a_ref[...], w_ref[...], preferred_element_type=f32)


def _proj_residual(x, a, w, *, tm):
    m, d = x.shape
    k = a.shape[1]
    vmem = 2 * 2 * tm * d * 4 + 2 * tm * k * 2 + k * d * 2 + tm * d * 4
    return pl.pallas_call(
        _proj_kernel,
        out_shape=jax.ShapeDtypeStruct((m, d), f32),
        grid=(m // tm,),
        in_specs=[pl.BlockSpec((tm, d), lambda i: (i, 0)),
                  pl.BlockSpec((tm, k), lambda i: (i, 0)),
                  _resident((k, d))],
        out_specs=pl.BlockSpec((tm, d), lambda i: (i, 0)),
        compiler_params=_params(("parallel",), vmem),
        name="proj_residual",
    )(x, a, w)


def _glu_kernel(x_ref, a_ref, wa_ref, wb_ref, o_ref):
    a = a_ref[...]
    p = jnp.dot(a, wa_ref[...], preferred_element_type=f32)
    q = jnp.dot(a, wb_ref[...], preferred_element_type=f32)
    o_ref[...] = x_ref[...] + p * jax.nn.sigmoid(q)


def _glu_residual(x, a, wa, wb, *, tm):
    m, d = x.shape
    k = a.shape[1]
    vmem = 2 * 2 * tm * d * 4 + 2 * tm * k * 2 + 2 * k * d * 2 + 3 * tm * d * 4
    return pl.pallas_call(
        _glu_kernel,
        out_shape=jax.ShapeDtypeStruct((m, d), f32),
        grid=(m // tm,),
        in_specs=[pl.BlockSpec((tm, d), lambda i: (i, 0)),
                  pl.BlockSpec((tm, k), lambda i: (i, 0)),
                  _resident((k, d)), _resident((k, d))],
        out_specs=pl.BlockSpec((tm, d), lambda i: (i, 0)),
        compiler_params=_params(("parallel",), vmem),
        name="glu_residual",
    )(x, a, wa, wb)


def _pool_kernel(x_ref, prev_ref, g_ref, w_ref, sc_ref, o_ref, buf_ref, ext_ref, lvl_ref, *, tt, start_pos):
    t = pl.program_id(1)
    lo = POOL_PAD
    top = POOL_PAD + POOL_HALO

    @pl.when(t == 0)
    def _():
        ext_ref[0:lo, :] = jnp.zeros((lo, ext_ref.shape[1]), f32)
        lvl_ref[:, 0:lo, :] = jnp.zeros((lvl_ref.shape[0], lo, lvl_ref.shape[2]), f32)
        ext_ref[lo:top, :] = prev_ref[0]

    @pl.when(t > 0)
    def _():
        ext_ref[lo:top, :] = ext_ref[tt + lo:tt + top, :]

    x = x_ref[0]
    h = _rms(x, g_ref[...])
    ext_ref[top:top + tt, :] = h
    buf_ref[0] = ext_ref[tt + lo:tt + top, :]

    gd = x.shape[1] // len(POOL_WINDOWS)
    pos = start_pos + t * tt + lax.broadcasted_iota(jnp.int32, (tt, 1), 0)
    for gi, w in enumerate(POOL_WINDOWS):
        sl = slice(gi * gd, (gi + 1) * gd)
        read = lambda a, b: ext_ref[a:b, sl]
        span, level = 1, 0
        while 2 * span < w:
            lvl_ref[level % 2, lo:top + tt, :] = read(lo, top + tt) + read(lo - span, top + tt - span)
            read = functools.partial(lambda k, a, b: lvl_ref[k, a:b, :], level % 2)
            span, level = 2 * span, level + 1
        win = read(top, top + tt) + read(top - span, top + tt - span)
        cnt = jnp.minimum(pos + 1, w).astype(f32)
        p = win / cnt - h[:, sl]
        y = jnp.dot(p.astype(bf16), w_ref[gi], preferred_element_type=f32)
        o_ref[0, :, sl] = x[:, sl] + y * sc_ref[:, sl]


def _pool_layer(x, prev16, g, w_pool, scale, *, tt, start_pos):
    b, t, d = x.shape
    ng, gd = w_pool.shape[0], w_pool.shape[1]
    ext_rows = POOL_PAD + POOL_HALO + tt
    vmem = 2 * 2 * tt * d * 4 + ext_rows * (d + 2 * gd) * 4 + ng * gd * gd * 2 + 4 * tt * d * 4
    return pl.pallas_call(
        functools.partial(_pool_kernel, tt=tt, start_pos=start_pos),
        out_shape=(jax.ShapeDtypeStruct((b, t, d), f32), jax.ShapeDtypeStruct((b, POOL_HALO, d), f32)),
        grid=(b, t // tt),
        in_specs=[pl.BlockSpec((1, tt, d), lambda i, j: (i, j, 0)),
                  pl.BlockSpec((1, POOL_HALO, d), lambda i, j: (i, 0, 0)),
                  pl.BlockSpec((1, d), lambda i, j: (0, 0)),
                  _resident((ng, gd, gd)),
                  pl.BlockSpec((1, d), lambda i, j: (0, 0))],
        out_specs=(pl.BlockSpec((1, tt, d), lambda i, j: (i, j, 0)),
                   pl.BlockSpec((1, POOL_HALO, d), lambda i, j: (i, 0, 0))),
        scratch_shapes=[pltpu.VMEM((ext_rows, d), f32), pltpu.VMEM((2, ext_rows, gd), f32)],
        compiler_params=_params(("parallel", "arbitrary"), vmem),
        name="pool_layer",
    )(x, prev16, g.reshape(1, d), w_pool, scale.reshape(1, d))


def _s5_kernel(x_ref, g_ref, h0r_ref, h0i_ref, are_ref, aim_ref, bblk_ref, cblk_ref, d_ref,
               gout_ref, sr_ref, si_ref, u_ref, bu_ref, gel_ref, *, nb, tt):
    t = pl.program_id(0)
    npack = u_ref.shape[0]
    nsl = (S5_PACK * S5_STATE) // LANES

    @pl.when(t == 0)
    def _():
        sr_ref[...] = h0r_ref[...]
        si_ref[...] = h0i_ref[...]

    g = g_ref[...]
    for b in range(nb):
        u = _rms(x_ref[b], g)
        for p in range(npack):
            u_ref.at[p][pl.ds(b, tt, stride=nb), :] = u[:, p * LANES:(p + 1) * LANES]

    def pack_body(p, carry):
        u = u_ref[p]
        bu = jnp.dot(u.astype(bf16), bblk_ref[p], preferred_element_type=f32)
        for j in range(2 * nsl):
            bu_ref[j] = bu[:, j * LANES:(j + 1) * LANES]
        a_re = are_ref[p]
        a_im = aim_ref[p]
        ar = [jnp.broadcast_to(a_re[:, j * LANES:(j + 1) * LANES], (nb, LANES)) for j in range(nsl)]
        ai = [jnp.broadcast_to(a_im[:, j * LANES:(j + 1) * LANES], (nb, LANES)) for j in range(nsl)]
        s0r = sr_ref[p]
        s0i = si_ref[p]
        init = (tuple(s0r[:, j * LANES:(j + 1) * LANES] for j in range(nsl)),
                tuple(s0i[:, j * LANES:(j + 1) * LANES] for j in range(nsl)))

        def step(k, s):
            s_re, s_im = s
            new_re, new_im = [], []
            for j in range(nsl):
                rows = pl.ds(pl.multiple_of(k * nb, nb), nb)
                b_re = bu_ref.at[j][rows, :]
                b_im = bu_ref.at[nsl + j][rows, :]
                n_re = ar[j] * s_re[j] - ai[j] * s_im[j] + b_re
                n_im = ar[j] * s_im[j] + ai[j] * s_re[j] + b_im
                bu_ref.at[j][rows, :] = n_re
                bu_ref.at[nsl + j][rows, :] = n_im
                new_re.append(n_re)
                new_im.append(n_im)
            return tuple(new_re), tuple(new_im)

        s_re, s_im = lax.fori_loop(0, tt, step, init, unroll=SCAN_UNROLL)
        sr_ref[p] = jnp.concatenate(s_re, axis=-1)
        si_ref[p] = jnp.concatenate(s_im, axis=-1)
        states = jnp.concatenate([bu_ref[j] for j in range(2 * nsl)], axis=-1).astype(bf16)
        y = jnp.dot(states, cblk_ref[p], preferred_element_type=f32) + d_ref[p] * u
        gel_ref[p] = 0.5 * y * (1.0 + lax.erf(y * math.sqrt(0.5)))
        return carry

    lax.fori_loop(0, npack, pack_body, 0)
    for b in range(nb):
        for p in range(npack):
            gout_ref[b, :, p * LANES:(p + 1) * LANES] = gel_ref.at[p][pl.ds(b, tt, stride=nb), :].astype(bf16)


def _s5_scan(x, g, h0_re, h0_im, a_re, a_im, bblk, cblk, dvec, *, tt):
    b, t, d = x.shape
    npack = d // LANES
    sw = S5_PACK * S5_STATE
    rows = b * tt
    vmem = (2 * rows * d * 4 + rows * d * 4 + rows * 2 * sw * 4 + rows * d * 4 + 2 * rows * d * 2
            + 2 * npack * LANES * 2 * sw * 2 + 3 * rows * 2 * sw * 4)
    st_spec = pl.BlockSpec((npack, b, sw), lambda i: (0, 0, 0))
    return pl.pallas_call(
        functools.partial(_s5_kernel, nb=b, tt=tt),
        out_shape=(jax.ShapeDtypeStruct((b, t, d), bf16),
                   jax.ShapeDtypeStruct((npack, b, sw), f32),
                   jax.ShapeDtypeStruct((npack, b, sw), f32)),
        grid=(t // tt,),
        in_specs=[pl.BlockSpec((b, tt, d), lambda i: (0, i, 0)),
                  pl.BlockSpec((1, d), lambda i: (0, 0)),
                  st_spec, st_spec,
                  _resident((npack, 1, sw)), _resident((npack, 1, sw)),
                  _resident((npack, LANES, 2 * sw)), _resident((npack, 2 * sw, LANES)),
                  _resident((npack, 1, LANES))],
        out_specs=(pl.BlockSpec((b, tt, d), lambda i: (0, i, 0)), st_spec, st_spec),
        scratch_shapes=[pltpu.VMEM((npack, rows, LANES), f32),
                        pltpu.VMEM((2 * sw // LANES, rows, LANES), f32),
                        pltpu.VMEM((npack, rows, LANES), f32)],
        compiler_params=_params(("arbitrary",), vmem),
        name="s5_scan",
    )(x, g.reshape(1, d), h0_re, h0_im, a_re, a_im, bblk, cblk, dvec)


def _s5_discretise(a_re, a_im, b_re, b_im, c_re, c_im, log_dt):
    g, n = a_re.shape
    c = b_re.shape[-1]
    npack = g // S5_PACK
    dt = jnp.exp(log_dt)[:, None]
    mag = jnp.exp(dt * a_re)
    abar_re = mag * jnp.cos(dt * a_im)
    abar_im = mag * jnp.sin(dt * a_im)
    den = a_re * a_re + a_im * a_im
    nr = abar_re - 1.0
    f_re = (nr * a_re + abar_im * a_im) / den
    f_im = (abar_im * a_re - nr * a_im) / den
    bbar_re = f_re[..., None] * b_re - f_im[..., None] * b_im
    bbar_im = f_re[..., None] * b_im + f_im[..., None] * b_re
    eye = jnp.eye(S5_PACK, dtype=f32)

    def in_blocks(m):
        m = jnp.swapaxes(m, 1, 2).reshape(npack, S5_PACK, c, n)
        return jnp.einsum('pgcn,gh->pgchn', m, eye).reshape(npack, S5_PACK * c, S5_PACK * n)

    def out_blocks(m):
        m = jnp.swapaxes(m, 1, 2).reshape(npack, S5_PACK, n, c)
        return jnp.einsum('pgnc,gh->pgnhc', m, eye).reshape(npack, S5_PACK * n, S5_PACK * c)

    bblk = jnp.concatenate([in_blocks(bbar_re), in_blocks(bbar_im)], axis=2).astype(bf16)
    cblk = jnp.concatenate([out_blocks(c_re), -out_blocks(c_im)], axis=1).astype(bf16)
    pack = lambda v: v.reshape(npack, 1, S5_PACK * n)
    return pack(abar_re), pack(abar_im), bblk, cblk


def _pack_state(s):
    b, g, n = s.shape
    return jnp.swapaxes(s.reshape(b, g // S5_PACK, S5_PACK * n), 0, 1)


def _unpack_state(s):
    npack, b, sw = s.shape
    return jnp.swapaxes(s, 0, 1).reshape(b, npack * S5_PACK, S5_STATE)


def _qkv_kernel(x_ref, g_ref, w_ref, gq_ref, gk_ref, q_ref, k_ref, v_ref, h_ref, *, nblk):
    n = pl.program_id(1)

    @pl.when(n == 0)
    def _():
        h_ref[...] = _rms(x_ref[...], g_ref[...]).astype(bf16)

    r = jnp.dot(h_ref[...], w_ref[...], preferred_element_type=f32)

    def head_norm(gain):
        heads = [r[:, i * SB_HEAD_DIM:(i + 1) * SB_HEAD_DIM] for i in range(r.shape[1] // SB_HEAD_DIM)]
        return jnp.concatenate([_rms(c, gain) for c in heads], axis=-1)

    @pl.when(n < nblk)
    def _():
        q_ref[...] = head_norm(gq_ref[...]).astype(bf16)

    @pl.when((n >= nblk) & (n < 2 * nblk))
    def _():
        k_ref[...] = head_norm(gk_ref[...])

    @pl.when(n >= 2 * nblk)
    def _():
        v_ref[...] = r


def _qkv_proj(x, g, w_qkv, g_q, g_k, *, tm, tn):
    m, d = x.shape
    nblk = d // tn
    vmem = 2 * tm * d * 4 + tm * d * 2 + 2 * d * tn * 2 + 2 * 3 * tm * tn * 4 + 3 * tm * tn * 4

    def out_spec(which):
        return pl.BlockSpec((tm, tn), lambda i, n: (i, jnp.clip(n - which * nblk, 0, nblk - 1)))

    return pl.pallas_call(
        functools.partial(_qkv_kernel, nblk=nblk),
        out_shape=(jax.ShapeDtypeStruct((m, d), bf16), jax.ShapeDtypeStruct((m, d), f32),
                   jax.ShapeDtypeStruct((m, d), f32)),
        grid=(m // tm, 3 * nblk),
        in_specs=[pl.BlockSpec((tm, d), lambda i, n: (i, 0)),
                  pl.BlockSpec((1, d), lambda i, n: (0, 0)),
                  pl.BlockSpec((d, tn), lambda i, n: (0, n)),
                  pl.BlockSpec((1, SB_HEAD_DIM), lambda i, n: (0, 0)),
                  pl.BlockSpec((1, SB_HEAD_DIM), lambda i, n: (0, 0))],
        out_specs=(out_spec(0), out_spec(1), out_spec(2)),
        scratch_shapes=[pltpu.VMEM((tm, d), bf16)],
        compiler_params=_params(("parallel", "arbitrary"), vmem),
        name="qkv_proj",
    )(x, g.reshape(1, d), w_qkv, g_q.reshape(1, SB_HEAD_DIM), g_k.reshape(1, SB_HEAD_DIM))


def _later_matrix(tk):
    j = lax.broadcasted_iota(jnp.int32, (tk, tk), 0)
    s = lax.broadcasted_iota(jnp.int32, (tk, tk), 1)
    return jnp.where(j > s, 1.0, 0.0).astype(bf16)


def _sb_logw2(z2, later_m, mask):
    chunk = later_m.shape[0]
    l = jnp.log(1.0 + jnp.exp2(-jnp.abs(z2))) * LOG2E
    sp = jnp.maximum(z2, 0.0) + l
    if mask is not None:
        sp = jnp.where(mask, sp, 0.0)
    sp16 = sp.astype(bf16)
    nchunk = z2.shape[1] // chunk
    later, total = [None] * nchunk, None
    for c in range(nchunk - 1, -1, -1):
        cols = slice(c * chunk, (c + 1) * chunk)
        in_chunk = jnp.dot(sp16[:, cols], later_m, preferred_element_type=f32)
        later[c] = in_chunk if total is None else in_chunk + total
        rowsum = jnp.sum(sp[:, cols], axis=-1, keepdims=True)
        total = rowsum if total is None else total + rowsum
    later = later[0] if nchunk == 1 else jnp.concatenate(later, axis=1)
    return jnp.minimum(z2, 0.0) - l - later, total


def _sb_prompt_kernel(bias_ref, q_ref, k_ref, v_ref, o_ref, kb_ref, vb_ref, *, tq):
    kb_ref[...] = k_ref[0].astype(bf16)
    vb_ref[...] = v_ref[0].astype(bf16)
    bias2 = bias_ref[pl.program_id(1)] * LOG2E
    scale2 = LOG2E / math.sqrt(SB_HEAD_DIM)
    later_m = _later_matrix(min(SB_CHUNK, tq))
    r = lax.broadcasted_iota(jnp.int32, (tq, tq), 0)
    c = lax.broadcasted_iota(jnp.int32, (tq, tq), 1)
    causal = c < r
    for qi in range(q_ref.shape[1] // tq):
        q = q_ref[0, qi * tq:(qi + 1) * tq, :]
        acc = jnp.zeros((tq, SB_HEAD_DIM), f32)
        carry = None
        for kb in range(qi, -1, -1):
            rows = slice(kb * tq, (kb + 1) * tq)
            mask = causal if kb == qi else None
            z2 = lax.dot_general(q, kb_ref[rows, :], _NT, preferred_element_type=f32) * scale2 + bias2
            logw, tot = _sb_logw2(z2, later_m, mask)
            if carry is not None:
                logw = logw - carry
            w = jnp.exp2(logw)
            if mask is not None:
                w = jnp.where(mask, w, 0.0)
            acc = acc + jnp.dot(w.astype(bf16), vb_ref[rows, :], preferred_element_type=f32)
            carry = tot if carry is None else carry + tot
        o_ref[0, qi * tq:(qi + 1) * tq, :] = acc.astype(bf16)


def _sb_prompt(q, k, v, bias, *, tq):
    b, t, d = q.shape
    nh = d // SB_HEAD_DIM
    head = pl.BlockSpec((1, t, SB_HEAD_DIM), lambda i, h: (i, 0, h))
    vmem = 2 * 2 * t * SB_HEAD_DIM * 4 + 2 * 2 * t * SB_HEAD_DIM * 2 + 2 * t * SB_HEAD_DIM * 2 + 24 * tq * tq * 4
    return pl.pallas_call(
        functools.partial(_sb_prompt_kernel, tq=tq),
        out_shape=jax.ShapeDtypeStruct((b, t, d), bf16),
        grid=(b, nh),
        in_specs=[pl.BlockSpec(memory_space=pltpu.SMEM), head, head, head],
        out_specs=head,
        scratch_shapes=[pltpu.VMEM((t, SB_HEAD_DIM), bf16), pltpu.VMEM((t, SB_HEAD_DIM), bf16)],
        compiler_params=_params(("parallel", "parallel"), vmem),
        name="sb_prompt",
    )(bias, q, k, v)


def _sb_decode_kernel(pt_ref, q_ref, kn_ref, vn_ref, ck_ref, cv_ref, bias_ref, o_ref,
                      qbd_ref, qstage_ref, acc_ref, carry_ref, k2d_ref, v2d_ref, later_ref,
                      kbuf_ref, vbuf_ref, sem_ref, *, ts, nh, npg, layer, n_pages):
    b = pl.program_id(0)
    i = pl.program_id(1)
    nsteps = pl.num_programs(1) - 1
    nrow = nh * ts
    dh = SB_HEAD_DIM
    scale2 = LOG2E / math.sqrt(dh)

    def page_copies(step, buf):
        copies = []
        for s in range(npg):
            page = pt_ref[b, n_pages - step * npg + s]
            for h in range(nh):
                copies.append(pltpu.make_async_copy(ck_ref.at[layer, page, :, h, :], kbuf_ref.at[buf, s, h],
                                                    sem_ref.at[buf, 0]))
                copies.append(pltpu.make_async_copy(cv_ref.at[layer, page, :, h, :], vbuf_ref.at[buf, s, h],
                                                    sem_ref.at[buf, 1]))
        return copies

    def block(k2d, v2d, later_m, mask):
        z2 = (lax.dot_general(qbd_ref[...], k2d, _NT, preferred_element_type=f32) * scale2
              + bias_ref[...] * LOG2E)
        logw, tot = _sb_logw2(z2, later_m, mask)
        w = jnp.exp2(logw - carry_ref[...])
        if mask is not None:
            w = jnp.where(mask, w, 0.0)
        carry_ref[...] += tot
        out = jnp.dot(w.astype(bf16), v2d, preferred_element_type=f32)
        for h in range(nh):
            acc_ref[h * ts:(h + 1) * ts, :] += out[h * ts:(h + 1) * ts, h * dh:(h + 1) * dh]

    @pl.when(i == 0)
    def _():
        for c in page_copies(1, 1):
            c.start()
        qstage_ref[...] = jnp.zeros_like(qstage_ref)
        for h in range(nh):
            qstage_ref[h * ts:(h + 1) * ts, h * dh:(h + 1) * dh] = q_ref[0, :, h * dh:(h + 1) * dh]
        qbd_ref[...] = qstage_ref[...].astype(bf16)
        acc_ref[...] = jnp.zeros_like(acc_ref)
        carry_ref[...] = jnp.zeros_like(carry_ref)
        later_ref[...] = _later_matrix(later_ref.shape[0])
        nk = kn_ref.shape[1]
        t_of_row = lax.broadcasted_iota(jnp.int32, (nrow, nk), 0) % ts
        j = lax.broadcasted_iota(jnp.int32, (nrow, nk), 1)
        chunk = min(nk, later_ref.shape[0])
        block(kn_ref[0].astype(bf16), vn_ref[0].astype(bf16), later_ref[0:chunk, 0:chunk], j < t_of_row)

    def page_step(buf):
        @pl.when(i < nsteps)
        def _():
            for c in page_copies(i + 1, 1 - buf):
                c.start()
        for c in page_copies(i, buf):
            c.wait()
        for s in range(npg):
            for h in range(nh):
                keys = slice(s * PAGE_SIZE, (s + 1) * PAGE_SIZE)
                k2d_ref[keys, h * dh:(h + 1) * dh] = kbuf_ref[buf, s, h].astype(bf16)
                v2d_ref[keys, h * dh:(h + 1) * dh] = vbuf_ref[buf, s, h].astype(bf16)
        block(k2d_ref[...], v2d_ref[...], later_ref[...], None)

    @pl.when((i > 0) & (i % 2 == 1))
    def _():
        page_step(1)

    @pl.when((i > 0) & (i % 2 == 0))
    def _():
        page_step(0)

    @pl.when(i == nsteps)
    def _():
        for h in range(nh):
            o_ref[0, :, h * dh:(h + 1) * dh] = acc_ref[h * ts:(h + 1) * ts, :]


def _sb_decode(q, k_new, v_new, cache_k, cache_v, page_table, bias_rows, *, npg, layer):
    b, ts, d = q.shape
    nh = d // SB_HEAD_DIM
    n_pages = page_table.shape[1]
    nrow = nh * ts
    nkeys = npg * PAGE_SIZE
    chunk = min(SB_CHUNK, nkeys)
    per_b = lambda i, j, pt: (i, 0, 0)
    slabs = (2, npg, nh, PAGE_SIZE, SB_HEAD_DIM)
    vmem = (2 * 2 * npg * PAGE_SIZE * d * 4 + 2 * 2 * PAGE_SIZE * d * 4 + 2 * nkeys * d * 2
            + nrow * d * 6 + chunk * chunk * 2 + 4 * nrow * d * 4)
    return pl.pallas_call(
        functools.partial(_sb_decode_kernel, ts=ts, nh=nh, npg=npg, layer=layer, n_pages=n_pages),
        out_shape=jax.ShapeDtypeStruct((b, ts, d), f32),
        grid_spec=pltpu.PrefetchScalarGridSpec(
            num_scalar_prefetch=1,
            grid=(b, n_pages // npg + 1),
            in_specs=[pl.BlockSpec((1, ts, d), per_b),
                      pl.BlockSpec((1, PAGE_SIZE, d), per_b),
                      pl.BlockSpec((1, PAGE_SIZE, d), per_b),
                      pl.BlockSpec(memory_space=pl.ANY),
                      pl.BlockSpec(memory_space=pl.ANY),
                      pl.BlockSpec((nrow, 1), lambda i, j, pt: (0, 0))],
            out_specs=pl.BlockSpec((1, ts, d), per_b),
            scratch_shapes=[pltpu.VMEM((nrow, d), bf16), pltpu.VMEM((nrow, d), f32),
                            pltpu.VMEM((nrow, SB_HEAD_DIM), f32), pltpu.VMEM((nrow, 1), f32),
                            pltpu.VMEM((nkeys, d), bf16), pltpu.VMEM((nkeys, d), bf16),
                            pltpu.VMEM((chunk, chunk), bf16),
                            pltpu.VMEM(slabs, f32), pltpu.VMEM(slabs, f32),
                            pltpu.SemaphoreType.DMA((2, 2))]),
        compiler_params=_params(("arbitrary", "arbitrary"), vmem),
        name="sb_decode",
    )(page_table, q, k_new, v_new, cache_k, cache_v, bias_rows)


def kernel(x_prompt, x_sample, cache_pool, state_ssm_re, state_ssm_im, cache_k, cache_v, page_table,
           norm_mix, norm_ffn, w_ffn_gate, w_ffn_up, w_ffn_down, w_pool, pool_scale,
           ssm_a_re, ssm_a_im, ssm_b_re, ssm_b_im, ssm_c_re, ssm_c_im, ssm_d, ssm_log_dt,
           w_glu_a, w_glu_b, w_qkv, w_o, sb_q_norm, sb_k_norm, sb_bias):
    bp, tp, d = x_prompt.shape
    bs, ts, _ = x_sample.shape
    depth = norm_mix.shape[0]
    nh = d // SB_HEAD_DIM
    mp, ms = bp * tp, bs * ts
    past_len = page_table.shape[1] * PAGE_SIZE
    n_phys = cache_k.shape[1]
    tm = TM

    xp, xs = x_prompt, x_sample
    wg, wu, wd = w_ffn_gate.astype(bf16), w_ffn_up.astype(bf16), w_ffn_down.astype(bf16)
    pool_p, pool_s = [], []
    ssm_p, ssm_s = [], []
    kv_p, kv_s = [], []
    for i in range(depth):
        kind = i % N_MIXERS
        j = i // N_MIXERS
        if kind == 0:
            w = w_pool[j].astype(bf16)
            zero_prev = jnp.zeros((bp, POOL_HALO, d), f32)
            prev = jnp.pad(cache_pool[j], ((0, 0), (1, 0), (0, 0)))
            xp, buf_p = _pool_layer(xp, zero_prev, norm_mix[i], w, pool_scale[j], tt=POOL_TT, start_pos=0)
            xs, buf_s = _pool_layer(xs, prev, norm_mix[i], w, pool_scale[j], tt=ts, start_pos=past_len)
            pool_p.append(buf_p[:, 1:])
            pool_s.append(buf_s[:, 1:])
        elif kind == 1:
            a_re, a_im, bblk, cblk = _s5_discretise(ssm_a_re[j], ssm_a_im[j], ssm_b_re[j], ssm_b_im[j],
                                                    ssm_c_re[j], ssm_c_im[j], ssm_log_dt[j])
            dvec = ssm_d[j].reshape(d // LANES, 1, LANES)
            zero_state = jnp.zeros((d // LANES, bp, S5_PACK * S5_STATE), f32)
            gp, sr_p, si_p = _s5_scan(xp, norm_mix[i], zero_state, zero_state, a_re, a_im, bblk, cblk, dvec,
                                      tt=S5_TT)
            gs, sr_s, si_s = _s5_scan(xs, norm_mix[i], _pack_state(state_ssm_re[j]), _pack_state(state_ssm_im[j]),
                                      a_re, a_im, bblk, cblk, dvec, tt=ts)
            wa, wb = w_glu_a[j].astype(bf16), w_glu_b[j].astype(bf16)
            xp = _glu_residual(xp.reshape(mp, d), gp.reshape(mp, d), wa, wb, tm=TM_GLU).reshape(bp, tp, d)
            xs = _glu_residual(xs.reshape(ms, d), gs.reshape(ms, d), wa, wb, tm=ms).reshape(bs, ts, d)
            ssm_p.append((_unpack_state(sr_p), _unpack_state(si_p)))
            ssm_s.append((_unpack_state(sr_s), _unpack_state(si_s)))
        else:
            wqkv, wo = w_qkv[j].astype(bf16), w_o[j].astype(bf16)
            q_p, k_p, v_p = _qkv_proj(xp.reshape(mp, d), norm_mix[i], wqkv, sb_q_norm[j], sb_k_norm[j],
                                      tm=tm, tn=TN_QKV)
            q_s, k_s, v_s = _qkv_proj(xs.reshape(ms, d), norm_mix[i], wqkv, sb_q_norm[j], sb_k_norm[j],
                                      tm=ms, tn=TN_QKV)
            o_p = _sb_prompt(q_p.reshape(bp, tp, d), k_p.reshape(bp, tp, d), v_p.reshape(bp, tp, d), sb_bias[j],
                             tq=TQ)
            pad_new = lambda a: jnp.pad(a.reshape(bs, ts, d), ((0, 0), (0, PAGE_SIZE - ts), (0, 0)))
            o_s = _sb_decode(q_s.reshape(bs, ts, d).astype(f32), pad_new(k_s), pad_new(v_s),
                             cache_k, cache_v, page_table, jnp.repeat(sb_bias[j], ts).reshape(nh * ts, 1),
                             npg=DECODE_PAGES, layer=j)
            xp = _proj_residual(xp.reshape(mp, d), o_p.reshape(mp, d), wo, tm=tm).reshape(bp, tp, d)
            xs = _proj_residual(xs.reshape(ms, d), o_s.reshape(ms, d).astype(bf16), wo, tm=ms).reshape(bs, ts, d)
            kv_p.append((k_p.reshape(bp, tp, nh, SB_HEAD_DIM), v_p.reshape(bp, tp, nh, SB_HEAD_DIM)))
            kv_s.append((k_s.reshape(bs, ts, nh, SB_HEAD_DIM), v_s.reshape(bs, ts, nh, SB_HEAD_DIM)))
        xp = _ffn(xp.reshape(mp, d), norm_ffn[i], wg, wu, wd, i, tm=TM_FFN).reshape(bp, tp, d)
        xs = _ffn(xs.reshape(ms, d), norm_ffn[i], wg, wu, wd, i, tm=ms).reshape(bs, ts, d)

    stack = lambda items: jnp.stack(items, axis=0)
    return (xp, xs, stack(pool_p), stack(pool_s),
            stack([s[0] for s in ssm_p]), stack([s[1] for s in ssm_p]),
            stack([s[0] for s in ssm_s]), stack([s[1] for s in ssm_s]),
            stack([kv[0] for kv in kv_p]), stack([kv[1] for kv in kv_p]),
            stack([kv[0] for kv in kv_s]), stack([kv[1] for kv in kv_s]))
```

```python
import functools
import math

import jax
import jax.numpy as jnp
from jax import lax
from jax.experimental import pallas as pl
from jax.experimental.pallas import tpu as pltpu

f32 = jnp.float32
bf16 = jnp.bfloat16

RMS_EPS = 1e-6
N_MIXERS = 3
POOL_WINDOWS = (2, 4, 8, 16)
POOL_HALO = 16
POOL_PAD = 8
S5_GROUP_SIZE = 16
S5_STATE = 64
S5_PACK = 8
SB_HEAD_DIM = 128
PAGE_SIZE = 128
LANES = 128
MIB = 1024 * 1024
VMEM_CAP = 56 * MIB

TM = 512
TM_FFN = 1024
TN_QKV = 1024
TM_GLU = 256
POOL_TT = 512
S5_TT = 128
TQ = 512
SB_CHUNK = 256
SCAN_UNROLL = 8
DECODE_PAGES = 4
LOG2E = 1.4426950408889634

_NT = (((1,), (1,)), ((), ()))


def _params(sem, vmem_bytes):
    return pltpu.CompilerParams(dimension_semantics=sem,
                                vmem_limit_bytes=int(min(VMEM_CAP, vmem_bytes + 12 * MIB)))


def _rms(x, g):
    return x * lax.rsqrt(jnp.mean(x * x, axis=-1, keepdims=True) + RMS_EPS) * g


def _resident(shape):
    nd = len(shape)
    return pl.BlockSpec(shape, lambda *_: (0,) * nd, pipeline_mode=pl.Buffered(1))


def _ffn_kernel(x_ref, g_ref, wg_ref, wu_ref, wd_ref, o_ref, h_ref):
    f = pl.program_id(1)

    @pl.when(f == 0)
    def _():
        x = x_ref[...]
        h_ref[...] = _rms(x, g_ref[...]).astype(bf16)
        o_ref[...] = x

    h = h_ref[...]
    a = jnp.dot(h, wg_ref[...], preferred_element_type=f32)
    b = jnp.dot(h, wu_ref[...], preferred_element_type=f32)
    act = (a * jax.nn.sigmoid(a) * b).astype(bf16)
    o_ref[...] += jnp.dot(act, wd_ref[...], preferred_element_type=f32)


def _ffn(x, g, wg, wu, wd, layer, *, tm, tf=512):
    m, d = x.shape
    nf = wg.shape[2] // tf
    vmem = 2 * 2 * tm * d * 4 + tm * d * 2 + 2 * 3 * d * tf * 2 + 3 * tm * tf * 4
    return pl.pallas_call(
        _ffn_kernel,
        out_shape=jax.ShapeDtypeStruct((m, d), f32),
        grid=(m // tm, nf),
        in_specs=[pl.BlockSpec((tm, d), lambda i, j: (i, 0)),
                  pl.BlockSpec((1, d), lambda i, j: (0, 0)),
                  pl.BlockSpec((None, d, tf), lambda i, j: (layer, 0, j)),
                  pl.BlockSpec((None, d, tf), lambda i, j: (layer, 0, j)),
                  pl.BlockSpec((None, tf, d), lambda i, j: (layer, j, 0))],
        out_specs=pl.BlockSpec((tm, d), lambda i, j: (i, 0)),
        scratch_shapes=[pltpu.VMEM((tm, d), bf16)],
        compiler_params=_params(("parallel", "arbitrary"), vmem),
        name="ffn",
    )(x, g.reshape(1, d), wg, wu, wd)


def _proj_kernel(x_ref, a_ref, w_ref, o_ref):
    o_ref[...] = x_ref[...] + jnp.dot(a_ref[...], w_ref[...], preferred_element_type=f32)


def _proj_residual(x, a, w, *, tm):
    m, d = x.shape
    k = a.shape[1]
    vmem = 2 * 2 * tm * d * 4 + 2 * tm * k * 2 + k * d * 2 + tm * d * 4
    return pl.pallas_call(
        _proj_kernel,
        out_shape=jax.ShapeDtypeStruct((m, d), f32),
        grid=(m // tm,),
        in_specs=[pl.BlockSpec((tm, d), lambda i: (i, 0)),
                  pl.BlockSpec((tm, k), lambda i: (i, 0)),
                  _resident((k, d))],
        out_specs=pl.BlockSpec((tm, d), lambda i: (i, 0)),
        compiler_params=_params(("parallel",), vmem),
        name="proj_residual",
    )(x, a, w)


def _glu_kernel(x_ref, a_ref, wa_ref, wb_ref, o_ref):
    a = a_ref[...]
    p = jnp.dot(a, wa_ref[...], preferred_element_type=f32)
    q = jnp.dot(a, wb_ref[...], preferred_element_type=f32)
    o_ref[...] = x_ref[...] + p * jax.nn.sigmoid(q)


def _glu_residual(x, a, wa, wb, *, tm):
    m, d = x.shape
    k = a.shape[1]
    vmem = 2 * 2 * tm * d * 4 + 2 * tm * k * 2 + 2 * k * d * 2 + 3 * tm * d * 4
    return pl.pallas_call(
        _glu_kernel,
        out_shape=jax.ShapeDtypeStruct((m, d), f32),
        grid=(m // tm,),
        in_specs=[pl.BlockSpec((tm, d), lambda i: (i, 0)),
                  pl.BlockSpec((tm, k), lambda i: (i, 0)),
                  _resident((k, d)), _resident((k, d))],
        out_specs=pl.BlockSpec((tm, d), lambda i: (i, 0)),
        compiler_params=_params(("parallel",), vmem),
        name="glu_residual",
    )(x, a, wa, wb)


def _pool_kernel(x_ref, prev_ref, g_ref, w_ref, sc_ref, o_ref, buf_ref, ext_ref, lvl_ref, *, tt, start_pos):
    t = pl.program_id(1)
    lo = POOL_PAD
    top = POOL_PAD + POOL_HALO

    @pl.when(t == 0)
    def _():
        ext_ref[0:lo, :] = jnp.zeros((lo, ext_ref.shape[1]), f32)
        lvl_ref[:, 0:lo, :] = jnp.zeros((lvl_ref.shape[0], lo, lvl_ref.shape[2]), f32)
        ext_ref[lo:top, :] = prev_ref[0]

    @pl.when(t > 0)
    def _():
        ext_ref[lo:top, :] = ext_ref[tt + lo:tt + top, :]

    x = x_ref[0]
    h = _rms(x, g_ref[...])
    ext_ref[top:top + tt, :] = h
    buf_ref[0] = ext_ref[tt + lo:tt + top, :]

    gd = x.shape[1] // len(POOL_WINDOWS)
    pos = start_pos + t * tt + lax.broadcasted_iota(jnp.int32, (tt, 1), 0)
    for gi, w in enumerate(POOL_WINDOWS):
        sl = slice(gi * gd, (gi + 1) * gd)
        read = lambda a, b: ext_ref[a:b, sl]
        span, level = 1, 0
        while 2 * span < w:
            lvl_ref[level % 2, lo:top + tt, :] = read(lo, top + tt) + read(lo - span, top + tt - span)
            read = functools.partial(lambda k, a, b: lvl_ref[k, a:b, :], level % 2)
            span, level = 2 * span, level + 1
        win = read(top, top + tt) + read(top - span, top + tt - span)
        cnt = jnp.minimum(pos + 1, w).astype(f32)
        p = win / cnt - h[:, sl]
        y = jnp.dot(p.astype(bf16), w_ref[gi], preferred_element_type=f32)
        o_ref[0, :, sl] = x[:, sl] + y * sc_ref[:, sl]


def _pool_layer(x, prev16, g, w_pool, scale, *, tt, start_pos):
    b, t, d = x.shape
    ng, gd = w_pool.shape[0], w_pool.shape[1]
    ext_rows = POOL_PAD + POOL_HALO + tt
    vmem = 2 * 2 * tt * d * 4 + ext_rows * (d + 2 * gd) * 4 + ng * gd * gd * 2 + 4 * tt * d * 4
    return pl.pallas_call(
        functools.partial(_pool_kernel, tt=tt, start_pos=start_pos),
        out_shape=(jax.ShapeDtypeStruct((b, t, d), f32), jax.ShapeDtypeStruct((b, POOL_HALO, d), f32)),
        grid=(b, t // tt),
        in_specs=[pl.BlockSpec((1, tt, d), lambda i, j: (i, j, 0)),
                  pl.BlockSpec((1, POOL_HALO, d), lambda i, j: (i, 0, 0)),
                  pl.BlockSpec((1, d), lambda i, j: (0, 0)),
                  _resident((ng, gd, gd)),
                  pl.BlockSpec((1, d), lambda i, j: (0, 0))],
        out_specs=(pl.BlockSpec((1, tt, d), lambda i, j: (i, j, 0)),
                   pl.BlockSpec((1, POOL_HALO, d), lambda i, j: (i, 0, 0))),
        scratch_shapes=[pltpu.VMEM((ext_rows, d), f32), pltpu.VMEM((2, ext_rows, gd), f32)],
        compiler_params=_params(("parallel", "arbitrary"), vmem),
        name="pool_layer",
    )(x, prev16, g.reshape(1, d), w_pool, scale.reshape(1, d))


def _s5_kernel(x_ref, g_ref, h0r_ref, h0i_ref, are_ref, aim_ref, bblk_ref, cblk_ref, d_ref,
               gout_ref, sr_ref, si_ref, u_ref, bu_ref, gel_ref, *, nb, tt):
    t = pl.program_id(0)
    npack = u_ref.shape[0]
    nsl = (S5_PACK * S5_STATE) // LANES

    @pl.when(t == 0)
    def _():
        sr_ref[...] = h0r_ref[...]
        si_ref[...] = h0i_ref[...]

    g = g_ref[...]
    for b in range(nb):
        u = _rms(x_ref[b], g)
        for p in range(npack):
            u_ref.at[p][pl.ds(b, tt, stride=nb), :] = u[:, p * LANES:(p + 1) * LANES]

    def pack_body(p, carry):
        u = u_ref[p]
        bu = jnp.dot(u.astype(bf16), bblk_ref[p], preferred_element_type=f32)
        for j in range(2 * nsl):
            bu_ref[j] = bu[:, j * LANES:(j + 1) * LANES]
        a_re = are_ref[p]
        a_im = aim_ref[p]
        ar = [jnp.broadcast_to(a_re[:, j * LANES:(j + 1) * LANES], (nb, LANES)) for j in range(nsl)]
        ai = [jnp.broadcast_to(a_im[:, j * LANES:(j + 1) * LANES], (nb, LANES)) for j in range(nsl)]
        s0r = sr_ref[p]
        s0i = si_ref[p]
        init = (tuple(s0r[:, j * LANES:(j + 1) * LANES] for j in range(nsl)),
                tuple(s0i[:, j * LANES:(j + 1) * LANES] for j in range(nsl)))

        def step(k, s):
            s_re, s_im = s
            new_re, new_im = [], []
            for j in range(nsl):
                rows = pl.ds(pl.multiple_of(k * nb, nb), nb)
                b_re = bu_ref.at[j][rows, :]
                b_im = bu_ref.at[nsl + j][rows, :]
                n_re = ar[j] * s_re[j] - ai[j] * s_im[j] + b_re
                n_im = ar[j] * s_im[j] + ai[j] * s_re[j] + b_im
                bu_ref.at[j][rows, :] = n_re
                bu_ref.at[nsl + j][rows, :] = n_im
                new_re.append(n_re)
                new_im.append(n_im)
            return tuple(new_re), tuple(new_im)

        s_re, s_im = lax.fori_loop(0, tt, step, init, unroll=SCAN_UNROLL)
        sr_ref[p] = jnp.concatenate(s_re, axis=-1)
        si_ref[p] = jnp.concatenate(s_im, axis=-1)
        states = jnp.concatenate([bu_ref[j] for j in range(2 * nsl)], axis=-1).astype(bf16)
        y = jnp.dot(states, cblk_ref[p], preferred_element_type=f32) + d_ref[p] * u
        gel_ref[p] = 0.5 * y * (1.0 + lax.erf(y * math.sqrt(0.5)))
        return carry

    lax.fori_loop(0, npack, pack_body, 0)
    for b in range(nb):
        for p in range(npack):
            gout_ref[b, :, p * LANES:(p + 1) * LANES] = gel_ref.at[p][pl.ds(b, tt, stride=nb), :].astype(bf16)


def _s5_scan(x, g, h0_re, h0_im, a_re, a_im, bblk, cblk, dvec, *, tt):
    b, t, d = x.shape
    npack = d // LANES
    sw = S5_PACK * S5_STATE
    rows = b * tt
    vmem = (2 * rows * d * 4 + rows * d * 4 + rows * 2 * sw * 4 + rows * d * 4 + 2 * rows * d * 2
            + 2 * npack * LANES * 2 * sw * 2 + 3 * rows * 2 * sw * 4)
    st_spec = pl.BlockSpec((npack, b, sw), lambda i: (0, 0, 0))
    return pl.pallas_call(
        functools.partial(_s5_kernel, nb=b, tt=tt),
        out_shape=(jax.ShapeDtypeStruct((b, t, d), bf16),
                   jax.ShapeDtypeStruct((npack, b, sw), f32),
                   jax.ShapeDtypeStruct((npack, b, sw), f32)),
        grid=(t // tt,),
        in_specs=[pl.BlockSpec((b, tt, d), lambda i: (0, i, 0)),
                  pl.BlockSpec((1, d), lambda i: (0, 0)),
                  st_spec, st_spec,
                  _resident((npack, 1, sw)), _resident((npack, 1, sw)),
                  _resident((npack, LANES, 2 * sw)), _resident((npack, 2 * sw, LANES)),
                  _resident((npack, 1, LANES))],
        out_specs=(pl.BlockSpec((b, tt, d), lambda i: (0, i, 0)), st_spec, st_spec),
        scratch_shapes=[pltpu.VMEM((npack, rows, LANES), f32),
                        pltpu.VMEM((2 * sw // LANES, rows, LANES), f32),
                        pltpu.VMEM((npack, rows, LANES), f32)],
        compiler_params=_params(("arbitrary",), vmem),
        name="s5_scan",
    )(x, g.reshape(1, d), h0_re, h0_im, a_re, a_im, bblk, cblk, dvec)


def _s5_discretise(a_re, a_im, b_re, b_im, c_re, c_im, log_dt):
    g, n = a_re.shape
    c = b_re.shape[-1]
    npack = g // S5_PACK
    dt = jnp.exp(log_dt)[:, None]
    mag = jnp.exp(dt * a_re)
    abar_re = mag * jnp.cos(dt * a_im)
    abar_im = mag * jnp.sin(dt * a_im)
    den = a_re * a_re + a_im * a_im
    nr = abar_re - 1.0
    f_re = (nr * a_re + abar_im * a_im) / den
    f_im = (abar_im * a_re - nr * a_im) / den
    bbar_re = f_re[..., None] * b_re - f_im[..., None] * b_im
    bbar_im = f_re[..., None] * b_im + f_im[..., None] * b_re
    eye = jnp.eye(S5_PACK, dtype=f32)

    def in_blocks(m):
        m = jnp.swapaxes(m, 1, 2).reshape(npack, S5_PACK, c, n)
        return jnp.einsum('pgcn,gh->pgchn', m, eye).reshape(npack, S5_PACK * c, S5_PACK * n)

    def out_blocks(m):
        m = jnp.swapaxes(m, 1, 2).reshape(npack, S5_PACK, n, c)
        return jnp.einsum('pgnc,gh->pgnhc', m, eye).reshape(npack, S5_PACK * n, S5_PACK * c)

    bblk = jnp.concatenate([in_blocks(bbar_re), in_blocks(bbar_im)], axis=2).astype(bf16)
    cblk = jnp.concatenate([out_blocks(c_re), -out_blocks(c_im)], axis=1).astype(bf16)
    pack = lambda v: v.reshape(npack, 1, S5_PACK * n)
    return pack(abar_re), pack(abar_im), bblk, cblk


def _pack_state(s):
    b, g, n = s.shape
    return jnp.swapaxes(s.reshape(b, g // S5_PACK, S5_PACK * n), 0, 1)


def _unpack_state(s):
    npack, b, sw = s.shape
    return jnp.swapaxes(s, 0, 1).reshape(b, npack * S5_PACK, S5_STATE)


def _qkv_kernel(x_ref, g_ref, w_ref, gq_ref, gk_ref, q_ref, k_ref, v_ref, h_ref, *, nblk):
    n = pl.program_id(1)

    @pl.when(n == 0)
    def _():
        h_ref[...] = _rms(x_ref[...], g_ref[...]).astype(bf16)

    r = jnp.dot(h_ref[...], w_ref[...], preferred_element_type=f32)

    def head_norm(gain):
        heads = [r[:, i * SB_HEAD_DIM:(i + 1) * SB_HEAD_DIM] for i in range(r.shape[1] // SB_HEAD_DIM)]
        return jnp.concatenate([_rms(c, gain) for c in heads], axis=-1)

    @pl.when(n < nblk)
    def _():
        q_ref[...] = head_norm(gq_ref[...]).astype(bf16)

    @pl.when((n >= nblk) & (n < 2 * nblk))
    def _():
        k_ref[...] = head_norm(gk_ref[...])

    @pl.when(n >= 2 * nblk)
    def _():
        v_ref[...] = r


def _qkv_proj(x, g, w_qkv, g_q, g_k, *, tm, tn):
    m, d = x.shape
    nblk = d // tn
    vmem = 2 * tm * d * 4 + tm * d * 2 + 2 * d * tn * 2 + 2 * 3 * tm * tn * 4 + 3 * tm * tn * 4

    def out_spec(which):
        return pl.BlockSpec((tm, tn), lambda i, n: (i, jnp.clip(n - which * nblk, 0, nblk - 1)))

    return pl.pallas_call(
        functools.partial(_qkv_kernel, nblk=nblk),
        out_shape=(jax.ShapeDtypeStruct((m, d), bf16), jax.ShapeDtypeStruct((m, d), f32),
                   jax.ShapeDtypeStruct((m, d), f32)),
        grid=(m // tm, 3 * nblk),
        in_specs=[pl.BlockSpec((tm, d), lambda i, n: (i, 0)),
                  pl.BlockSpec((1, d), lambda i, n: (0, 0)),
                  pl.BlockSpec((d, tn), lambda i, n: (0, n)),
                  pl.BlockSpec((1, SB_HEAD_DIM), lambda i, n: (0, 0)),
                  pl.BlockSpec((1, SB_HEAD_DIM), lambda i, n: (0, 0))],
        out_specs=(out_spec(0), out_spec(1), out_spec(2)),
        scratch_shapes=[pltpu.VMEM((tm, d), bf16)],
        compiler_params=_params(("parallel", "arbitrary"), vmem),
        name="qkv_proj",
    )(x, g.reshape(1, d), w_qkv, g_q.reshape(1, SB_HEAD_DIM), g_k.reshape(1, SB_HEAD_DIM))


def _later_matrix(tk):
    j = lax.broadcasted_iota(jnp.int32, (tk, tk), 0)
    s = lax.broadcasted_iota(jnp.int32, (tk, tk), 1)
    return jnp.where(j > s, 1.0, 0.0).astype(bf16)


def _sb_logw2(z2, later_m, mask):
    chunk = later_m.shape[0]
    l = jnp.log(1.0 + jnp.exp2(-jnp.abs(z2))) * LOG2E
    sp = jnp.maximum(z2, 0.0) + l
    if mask is not None:
        sp = jnp.where(mask, sp, 0.0)
    sp16 = sp.astype(bf16)
    nchunk = z2.shape[1] // chunk
    later, total = [None] * nchunk, None
    for c in range(nchunk - 1, -1, -1):
        cols = slice(c * chunk, (c + 1) * chunk)
        in_chunk = jnp.dot(sp16[:, cols], later_m, preferred_element_type=f32)
        later[c] = in_chunk if total is None else in_chunk + total
        rowsum = jnp.sum(sp[:, cols], axis=-1, keepdims=True)
        total = rowsum if total is None else total + rowsum
    later = later[0] if nchunk == 1 else jnp.concatenate(later, axis=1)
    return jnp.minimum(z2, 0.0) - l - later, total


def _sb_prompt_kernel(bias_ref, q_ref, k_ref, v_ref, o_ref, kb_ref, vb_ref, *, tq):
    kb_ref[...] = k_ref[0].astype(bf16)
    vb_ref[...] = v_ref[0].astype(bf16)
    bias2 = bias_ref[pl.program_id(1)] * LOG2E
    scale2 = LOG2E / math.sqrt(SB_HEAD_DIM)
    later_m = _later_matrix(min(SB_CHUNK, tq))
    r = lax.broadcasted_iota(jnp.int32, (tq, tq), 0)
    c = lax.broadcasted_iota(jnp.int32, (tq, tq), 1)
    causal = c < r
    for qi in range(q_ref.shape[1] // tq):
        q = q_ref[0, qi * tq:(qi + 1) * tq, :]
        acc = jnp.zeros((tq, SB_HEAD_DIM), f32)
        carry = None
        for kb in range(qi, -1, -1):
            rows = slice(kb * tq, (kb + 1) * tq)
            mask = causal if kb == qi else None
            z2 = lax.dot_general(q, kb_ref[rows, :], _NT, preferred_element_type=f32) * scale2 + bias2
            logw, tot = _sb_logw2(z2, later_m, mask)
            if carry is not None:
                logw = logw - carry
            w = jnp.exp2(logw)
            if mask is not None:
                w = jnp.where(mask, w, 0.0)
            acc = acc + jnp.dot(w.astype(bf16), vb_ref[rows, :], preferred_element_type=f32)
            carry = tot if carry is None else carry + tot
        o_ref[0, qi * tq:(qi + 1) * tq, :] = acc.astype(bf16)


def _sb_prompt(q, k, v, bias, *, tq):
    b, t, d = q.shape
    nh = d // SB_HEAD_DIM
    head = pl.BlockSpec((1, t, SB_HEAD_DIM), lambda i, h: (i, 0, h))
    vmem = 2 * 2 * t * SB_HEAD_DIM * 4 + 2 * 2 * t * SB_HEAD_DIM * 2 + 2 * t * SB_HEAD_DIM * 2 + 24 * tq * tq * 4
    return pl.pallas_call(
        functools.partial(_sb_prompt_kernel, tq=tq),
        out_shape=jax.ShapeDtypeStruct((b, t, d), bf16),
        grid=(b, nh),
        in_specs=[pl.BlockSpec(memory_space=pltpu.SMEM), head, head, head],
        out_specs=head,
        scratch_shapes=[pltpu.VMEM((t, SB_HEAD_DIM), bf16), pltpu.VMEM((t, SB_HEAD_DIM), bf16)],
        compiler_params=_params(("parallel", "parallel"), vmem),
        name="sb_prompt",
    )(bias, q, k, v)


def _sb_decode_kernel(pt_ref, q_ref, kn_ref, vn_ref, ck_ref, cv_ref, bias_ref, o_ref,
                      qbd_ref, qstage_ref, acc_ref, carry_ref, k2d_ref, v2d_ref, later_ref,
                      kbuf_ref, vbuf_ref, sem_ref, *, ts, nh, npg, layer, n_pages):
    b = pl.program_id(0)
    i = pl.program_id(1)
    nsteps = pl.num_programs(1) - 1
    nrow = nh * ts
    dh = SB_HEAD_DIM
    scale2 = LOG2E / math.sqrt(dh)

    def page_copies(step, buf):
        copies = []
        for s in range(npg):
            page = pt_ref[b, n_pages - step * npg + s]
            for h in range(nh):
                copies.append(pltpu.make_async_copy(ck_ref.at[layer, page, :, h, :], kbuf_ref.at[buf, s, h],
                                                    sem_ref.at[buf, 0]))
                copies.append(pltpu.make_async_copy(cv_ref.at[layer, page, :, h, :], vbuf_ref.at[buf, s, h],
                                                    sem_ref.at[buf, 1]))
        return copies

    def block(k2d, v2d, later_m, mask):
        z2 = (lax.dot_general(qbd_ref[...], k2d, _NT, preferred_element_type=f32) * scale2
              + bias_ref[...] * LOG2E)
        logw, tot = _sb_logw2(z2, later_m, mask)
        w = jnp.exp2(logw - carry_ref[...])
        if mask is not None:
            w = jnp.where(mask, w, 0.0)
        carry_ref[...] += tot
        out = jnp.dot(w.astype(bf16), v2d, preferred_element_type=f32)
        for h in range(nh):
            acc_ref[h * ts:(h + 1) * ts, :] += out[h * ts:(h + 1) * ts, h * dh:(h + 1) * dh]

    @pl.when(i == 0)
    def _():
        for c in page_copies(1, 1):
            c.start()
        qstage_ref[...] = jnp.zeros_like(qstage_ref)
        for h in range(nh):
            qstage_ref[h * ts:(h + 1) * ts, h * dh:(h + 1) * dh] = q_ref[0, :, h * dh:(h + 1) * dh]
        qbd_ref[...] = qstage_ref[...].astype(bf16)
        acc_ref[...] = jnp.zeros_like(acc_ref)
        carry_ref[...] = jnp.zeros_like(carry_ref)
        later_ref[...] = _later_matrix(later_ref.shape[0])
        nk = kn_ref.shape[1]
        t_of_row = lax.broadcasted_iota(jnp.int32, (nrow, nk), 0) % ts
        j = lax.broadcasted_iota(jnp.int32, (nrow, nk), 1)
        chunk = min(nk, later_ref.shape[0])
        block(kn_ref[0].astype(bf16), vn_ref[0].astype(bf16), later_ref[0:chunk, 0:chunk], j < t_of_row)

    def page_step(buf):
        @pl.when(i < nsteps)
        def _():
            for c in page_copies(i + 1, 1 - buf):
                c.start()
        for c in page_copies(i, buf):
            c.wait()
        for s in range(npg):
            for h in range(nh):
                keys = slice(s * PAGE_SIZE, (s + 1) * PAGE_SIZE)
                k2d_ref[keys, h * dh:(h + 1) * dh] = kbuf_ref[buf, s, h].astype(bf16)
                v2d_ref[keys, h * dh:(h + 1) * dh] = vbuf_ref[buf, s, h].astype(bf16)
        block(k2d_ref[...], v2d_ref[...], later_ref[...], None)

    @pl.when((i > 0) & (i % 2 == 1))
    def _():
        page_step(1)

    @pl.when((i > 0) & (i % 2 == 0))
    def _():
        page_step(0)

    @pl.when(i == nsteps)
    def _():
        for h in range(nh):
            o_ref[0, :, h * dh:(h + 1) * dh] = acc_ref[h * ts:(h + 1) * ts, :]


def _sb_decode(q, k_new, v_new, cache_k, cache_v, page_table, bias_rows, *, npg, layer):
    b, ts, d = q.shape
    nh = d // SB_HEAD_DIM
    n_pages = page_table.shape[1]
    nrow = nh * ts
    nkeys = npg * PAGE_SIZE
    chunk = min(SB_CHUNK, nkeys)
    per_b = lambda i, j, pt: (i, 0, 0)
    slabs = (2, npg, nh, PAGE_SIZE, SB_HEAD_DIM)
    vmem = (2 * 2 * npg * PAGE_SIZE * d * 4 + 2 * 2 * PAGE_SIZE * d * 4 + 2 * nkeys * d * 2
            + nrow * d * 6 + chunk * chunk * 2 + 4 * nrow * d * 4)
    return pl.pallas_call(
        functools.partial(_sb_decode_kernel, ts=ts, nh=nh, npg=npg, layer=layer, n_pages=n_pages),
        out_shape=jax.ShapeDtypeStruct((b, ts, d), f32),
        grid_spec=pltpu.PrefetchScalarGridSpec(
            num_scalar_prefetch=1,
            grid=(b, n_pages // npg + 1),
            in_specs=[pl.BlockSpec((1, ts, d), per_b),
                      pl.BlockSpec((1, PAGE_SIZE, d), per_b),
                      pl.BlockSpec((1, PAGE_SIZE, d), per_b),
                      pl.BlockSpec(memory_space=pl.ANY),
                      pl.BlockSpec(memory_space=pl.ANY),
                      pl.BlockSpec((nrow, 1), lambda i, j, pt: (0, 0))],
            out_specs=pl.BlockSpec((1, ts, d), per_b),
            scratch_shapes=[pltpu.VMEM((nrow, d), bf16), pltpu.VMEM((nrow, d), f32),
                            pltpu.VMEM((nrow, SB_HEAD_DIM), f32), pltpu.VMEM((nrow, 1), f32),
                            pltpu.VMEM((nkeys, d), bf16), pltpu.VMEM((nkeys, d), bf16),
                            pltpu.VMEM((chunk, chunk), bf16),
                            pltpu.VMEM(slabs, f32), pltpu.VMEM(slabs, f32),
                            pltpu.SemaphoreType.DMA((2, 2))]),
        compiler_params=_params(("arbitrary", "arbitrary"), vmem),
        name="sb_decode",
    )(page_table, q, k_new, v_new, cache_k, cache_v, bias_rows)


def kernel(x_prompt, x_sample, cache_pool, state_ssm_re, state_ssm_im, cache_k, cache_v, page_table,
           norm_mix, norm_ffn, w_ffn_gate, w_ffn_up, w_ffn_down, w_pool, pool_scale,
           ssm_a_re, ssm_a_im, ssm_b_re, ssm_b_im, ssm_c_re, ssm_c_im, ssm_d, ssm_log_dt,
           w_glu_a, w_glu_b, w_qkv, w_o, sb_q_norm, sb_k_norm, sb_bias):
    bp, tp, d = x_prompt.shape
    bs, ts, _ = x_sample.shape
    depth = norm_mix.shape[0]
    nh = d // SB_HEAD_DIM
    mp, ms = bp * tp, bs * ts
    past_len = page_table.shape[1] * PAGE_SIZE
    n_phys = cache_k.shape[1]
    tm = TM

    xp, xs = x_prompt, x_sample
    wg, wu, wd = w_ffn_gate.astype(bf16), w_ffn_up.astype(bf16), w_ffn_down.astype(bf16)
    pool_p, pool_s = [], []
    ssm_p, ssm_s = [], []
    kv_p, kv_s = [], []
    for i in range(depth):
        kind = i % N_MIXERS
        j = i // N_MIXERS
        if kind == 0:
            w = w_pool[j].astype(bf16)
            zero_prev = jnp.zeros((bp, POOL_HALO, d), f32)
            prev = jnp.pad(cache_pool[j], ((0, 0), (1, 0), (0, 0)))
            xp, buf_p = _pool_layer(xp, zero_prev, norm_mix[i], w, pool_scale[j], tt=POOL_TT, start_pos=0)
            xs, buf_s = _pool_layer(xs, prev, norm_mix[i], w, pool_scale[j], tt=ts, start_pos=past_len)
            pool_p.append(buf_p[:, 1:])
            pool_s.append(buf_s[:, 1:])
        elif kind == 1:
            a_re, a_im, bblk, cblk = _s5_discretise(ssm_a_re[j], ssm_a_im[j], ssm_b_re[j], ssm_b_im[j],
                                                    ssm_c_re[j], ssm_c_im[j], ssm_log_dt[j])
            dvec = ssm_d[j].reshape(d // LANES, 1, LANES)
            zero_state = jnp.zeros((d // LANES, bp, S5_PACK * S5_STATE), f32)
            gp, sr_p, si_p = _s5_scan(xp, norm_mix[i], zero_state, zero_state, a_re, a_im, bblk, cblk, dvec,
                                      tt=S5_TT)
            gs, sr_s, si_s = _s5_scan(xs, norm_mix[i], _pack_state(state_ssm_re[j]), _pack_state(state_ssm_im[j]),
                                      a_re, a_im, bblk, cblk, dvec, tt=ts)
            wa, wb = w_glu_a[j].astype(bf16), w_glu_b[j].astype(bf16)
            xp = _glu_residual(xp.reshape(mp, d), gp.reshape(mp, d), wa, wb, tm=TM_GLU).reshape(bp, tp, d)
            xs = _glu_residual(xs.reshape(ms, d), gs.reshape(ms, d), wa, wb, tm=ms).reshape(bs, ts, d)
            ssm_p.append((_unpack_state(sr_p), _unpack_state(si_p)))
            ssm_s.append((_unpack_state(sr_s), _unpack_state(si_s)))
        else:
            wqkv, wo = w_qkv[j].astype(bf16), w_o[j].astype(bf16)
            q_p, k_p, v_p = _qkv_proj(xp.reshape(mp, d), norm_mix[i], wqkv, sb_q_norm[j], sb_k_norm[j],
                                      tm=tm, tn=TN_QKV)
            q_s, k_s, v_s = _qkv_proj(xs.reshape(ms, d), norm_mix[i], wqkv, sb_q_norm[j], sb_k_norm[j],
                                      tm=ms, tn=TN_QKV)
            o_p = _sb_prompt(q_p.reshape(bp, tp, d), k_p.reshape(bp, tp, d), v_p.reshape(bp, tp, d), sb_bias[j],
                             tq=TQ)
            pad_new = lambda a: jnp.pad(a.reshape(bs, ts, d), ((0, 0), (0, PAGE_SIZE - ts), (0, 0)))
            o_s = _sb_decode(q_s.reshape(bs, ts, d).astype(f32), pad_new(k_s), pad_new(v_s),
                             cache_k, cache_v, page_table, jnp.repeat(sb_bias[j], ts).reshape(nh * ts, 1),
                             npg=DECODE_PAGES, layer=j)
            xp = _proj_residual(xp.reshape(mp, d), o_p.reshape(mp, d), wo, tm=tm).reshape(bp, tp, d)
            xs = _proj_residual(xs.reshape(ms, d), o_s.reshape(ms, d).astype(bf16), wo, tm=ms).reshape(bs, ts, d)
            kv_p.append((k_p.reshape(bp, tp, nh, SB_HEAD_DIM), v_p.reshape(bp, tp, nh, SB_HEAD_DIM)))
            kv_s.append((k_s.reshape(bs, ts, nh, SB_HEAD_DIM), v_s.reshape(bs, ts, nh, SB_HEAD_DIM)))
        xp = _ffn(xp.reshape(mp, d), norm_ffn[i], wg, wu, wd, i, tm=TM_FFN).reshape(bp, tp, d)
        xs = _ffn(xs.reshape(ms, d), norm_ffn[i], wg, wu, wd, i, tm=ms).reshape(bs, ts, d)

    stack = lambda items: jnp.stack(items, axis=0)
    return (xp, xs, stack(pool_p), stack(pool_s),
            stack([s[0] for s in ssm_p]), stack([s[1] for s in ssm_p]),
            stack([s[0] for s in ssm_s]), stack([s[1] for s in ssm_s]),
            stack([kv[0] for kv in kv_p]), stack([kv[1] for kv in kv_p]),
            stack([kv[0] for kv in kv_s]), stack([kv[1] for kv in kv_s]))
```

```python
import functools
import math

import jax
import jax.numpy as jnp
from jax import lax
from jax.experimental import pallas as pl
from jax.experimental.pallas import tpu as pltpu

f32 = jnp.float32
bf16 = jnp.bfloat16

RMS_EPS = 1e-6
N_MIXERS = 3
POOL_WINDOWS = (2, 4, 8, 16)
POOL_HALO = 16
POOL_PAD = 8
S5_GROUP_SIZE = 16
S5_STATE = 64
S5_PACK = 8
SB_HEAD_DIM = 128
PAGE_SIZE = 128
LANES = 128
MIB = 1024 * 1024
VMEM_CAP = 56 * MIB

TM = 512
TM_FFN = 1024
TM_QKV = 1024
TN_QKV = 512
TM_GLU = 256
POOL_TT = 512
S5_TT = 128
TQ = 512
SB_CHUNK = 256
SCAN_UNROLL = 8
DECODE_PAGES = 4
LOG2E = 1.4426950408889634

_NT = (((1,), (1,)), ((), ()))


def _params(sem, vmem_bytes):
    return pltpu.CompilerParams(dimension_semantics=sem,
                                vmem_limit_bytes=int(min(VMEM_CAP, vmem_bytes + 12 * MIB)))


def _rms(x, g):
    return x * lax.rsqrt(jnp.mean(x * x, axis=-1, keepdims=True) + RMS_EPS) * g


def _resident(shape):
    nd = len(shape)
    return pl.BlockSpec(shape, lambda *_: (0,) * nd, pipeline_mode=pl.Buffered(1))


def _ffn_kernel(x_ref, g_ref, wg_ref, wu_ref, wd_ref, o_ref, h_ref):
    f = pl.program_id(1)

    @pl.when(f == 0)
    def _():
        x = x_ref[...]
        h_ref[...] = _rms(x, g_ref[...]).astype(bf16)
        o_ref[...] = x

    h = h_ref[...]
    a = jnp.dot(h, wg_ref[...], preferred_element_type=f32)
    b = jnp.dot(h, wu_ref[...], preferred_element_type=f32)
    act = (a * jax.nn.sigmoid(a) * b).astype(bf16)
    o_ref[...] += jnp.dot(act, wd_ref[...], preferred_element_type=f32)


def _ffn(x, g, wg, wu, wd, layer, *, tm, tf=512):
    m, d = x.shape
    nf = wg.shape[2] // tf
    vmem = 2 * 2 * tm * d * 4 + tm * d * 2 + 2 * 3 * d * tf * 2 + 3 * tm * tf * 4
    return pl.pallas_call(
        _ffn_kernel,
        out_shape=jax.ShapeDtypeStruct((m, d), f32),
        grid=(m // tm, nf),
        in_specs=[pl.BlockSpec((tm, d), lambda i, j: (i, 0)),
                  pl.BlockSpec((1, d), lambda i, j: (0, 0)),
                  pl.BlockSpec((None, d, tf), lambda i, j: (layer, 0, j)),
                  pl.BlockSpec((None, d, tf), lambda i, j: (layer, 0, j)),
                  pl.BlockSpec((None, tf, d), lambda i, j: (layer, j, 0))],
        out_specs=pl.BlockSpec((tm, d), lambda i, j: (i, 0)),
        scratch_shapes=[pltpu.VMEM((tm, d), bf16)],
        compiler_params=_params(("parallel", "arbitrary"), vmem),
        name="ffn",
    )(x, g.reshape(1, d), wg, wu, wd)


def _proj_kernel(x_ref, a_ref, w_ref, o_ref):
    o_ref[...] = x_ref[...] + jnp.dot(a_ref[...], w_ref[...], preferred_element_type=f32)


def _proj_residual(x, a, w, *, tm):
    m, d = x.shape
    k = a.shape[1]
    vmem = 2 * 2 * tm * d * 4 + 2 * tm * k * 2 + k * d * 2 + tm * d * 4
    return pl.pallas_call(
        _proj_kernel,
        out_shape=jax.ShapeDtypeStruct((m, d), f32),
        grid=(m // tm,),
        in_specs=[pl.BlockSpec((tm, d), lambda i: (i, 0)),
                  pl.BlockSpec((tm, k), lambda i: (i, 0)),
                  _resident((k, d))],
        out_specs=pl.BlockSpec((tm, d), lambda i: (i, 0)),
        compiler_params=_params(("parallel",), vmem),
        name="proj_residual",
    )(x, a, w)


def _glu_kernel(x_ref, a_ref, wa_ref, wb_ref, o_ref):
    a = a_ref[...]
    p = jnp.dot(a, wa_ref[...], preferred_element_type=f32)
    q = jnp.dot(a, wb_ref[...], preferred_element_type=f32)
    o_ref[...] = x_ref[...] + p * jax.nn.sigmoid(q)


def _glu_residual(x, a, wa, wb, *, tm):
    m, d = x.shape
    k = a.shape[1]
    vmem = 2 * 2 * tm * d * 4 + 2 * tm * k * 2 + 2 * k * d * 2 + 3 * tm * d * 4
    return pl.pallas_call(
        _glu_kernel,
        out_shape=jax.ShapeDtypeStruct((m, d), f32),
        grid=(m // tm,),
        in_specs=[pl.BlockSpec((tm, d), lambda i: (i, 0)),
                  pl.BlockSpec((tm, k), lambda i: (i, 0)),
                  _resident((k, d)), _resident((k, d))],
        out_specs=pl.BlockSpec((tm, d), lambda i: (i, 0)),
        compiler_params=_params(("parallel",), vmem),
        name="glu_residual",
    )(x, a, wa, wb)


def _pool_kernel(x_ref, prev_ref, g_ref, w_ref, sc_ref, o_ref, buf_ref, ext_ref, lvl_ref, *, tt, start_pos):
    t = pl.program_id(1)
    lo = POOL_PAD
    top = POOL_PAD + POOL_HALO

    @pl.when(t == 0)
    def _():
        ext_ref[0:lo, :] = jnp.zeros((lo, ext_ref.shape[1]), f32)
        lvl_ref[:, 0:lo, :] = jnp.zeros((lvl_ref.shape[0], lo, lvl_ref.shape[2]), f32)
        ext_ref[lo:top, :] = prev_ref[0]

    @pl.when(t > 0)
    def _():
        ext_ref[lo:top, :] = ext_ref[tt + lo:tt + top, :]

    x = x_ref[0]
    h = _rms(x, g_ref[...])
    ext_ref[top:top + tt, :] = h
    buf_ref[0] = ext_ref[tt + lo:tt + top, :]

    gd = x.shape[1] // len(POOL_WINDOWS)
    pos = start_pos + t * tt + lax.broadcasted_iota(jnp.int32, (tt, 1), 0)
    for gi, w in enumerate(POOL_WINDOWS):
        sl = slice(gi * gd, (gi + 1) * gd)
        read = lambda a, b: ext_ref[a:b, sl]
        span, level = 1, 0
        while 2 * span < w:
            lvl_ref[level % 2, lo:top + tt, :] = read(lo, top + tt) + read(lo - span, top + tt - span)
            read = functools.partial(lambda k, a, b: lvl_ref[k, a:b, :], level % 2)
            span, level = 2 * span, level + 1
        win = read(top, top + tt) + read(top - span, top + tt - span)
        cnt = jnp.minimum(pos + 1, w).astype(f32)
        p = win / cnt - h[:, sl]
        y = jnp.dot(p.astype(bf16), w_ref[gi], preferred_element_type=f32)
        o_ref[0, :, sl] = x[:, sl] + y * sc_ref[:, sl]


def _pool_layer(x, prev16, g, w_pool, scale, *, tt, start_pos):
    b, t, d = x.shape
    ng, gd = w_pool.shape[0], w_pool.shape[1]
    ext_rows = POOL_PAD + POOL_HALO + tt
    vmem = 2 * 2 * tt * d * 4 + ext_rows * (d + 2 * gd) * 4 + ng * gd * gd * 2 + 4 * tt * d * 4
    return pl.pallas_call(
        functools.partial(_pool_kernel, tt=tt, start_pos=start_pos),
        out_shape=(jax.ShapeDtypeStruct((b, t, d), f32), jax.ShapeDtypeStruct((b, POOL_HALO, d), f32)),
        grid=(b, t // tt),
        in_specs=[pl.BlockSpec((1, tt, d), lambda i, j: (i, j, 0)),
                  pl.BlockSpec((1, POOL_HALO, d), lambda i, j: (i, 0, 0)),
                  pl.BlockSpec((1, d), lambda i, j: (0, 0)),
                  _resident((ng, gd, gd)),
                  pl.BlockSpec((1, d), lambda i, j: (0, 0))],
        out_specs=(pl.BlockSpec((1, tt, d), lambda i, j: (i, j, 0)),
                   pl.BlockSpec((1, POOL_HALO, d), lambda i, j: (i, 0, 0))),
        scratch_shapes=[pltpu.VMEM((ext_rows, d), f32), pltpu.VMEM((2, ext_rows, gd), f32)],
        compiler_params=_params(("parallel", "arbitrary"), vmem),
        name="pool_layer",
    )(x, prev16, g.reshape(1, d), w_pool, scale.reshape(1, d))


def _s5_kernel(x_ref, g_ref, h0r_ref, h0i_ref, are_ref, aim_ref, bblk_ref, cblk_ref, d_ref,
               gout_ref, sr_ref, si_ref, u_ref, bu_ref, gel_ref, *, nb, tt):
    t = pl.program_id(0)
    npack = u_ref.shape[0]
    nsl = (S5_PACK * S5_STATE) // LANES

    @pl.when(t == 0)
    def _():
        sr_ref[...] = h0r_ref[...]
        si_ref[...] = h0i_ref[...]

    g = g_ref[...]
    for b in range(nb):
        u = _rms(x_ref[b], g)
        for p in range(npack):
            u_ref.at[p][pl.ds(b, tt, stride=nb), :] = u[:, p * LANES:(p + 1) * LANES]

    def pack_body(p, carry):
        u = u_ref[p]
        bu = jnp.dot(u.astype(bf16), bblk_ref[p], preferred_element_type=f32)
        for j in range(2 * nsl):
            bu_ref[j] = bu[:, j * LANES:(j + 1) * LANES]
        a_re = are_ref[p]
        a_im = aim_ref[p]
        ar = [jnp.broadcast_to(a_re[:, j * LANES:(j + 1) * LANES], (nb, LANES)) for j in range(nsl)]
        ai = [jnp.broadcast_to(a_im[:, j * LANES:(j + 1) * LANES], (nb, LANES)) for j in range(nsl)]
        s0r = sr_ref[p]
        s0i = si_ref[p]
        init = (tuple(s0r[:, j * LANES:(j + 1) * LANES] for j in range(nsl)),
                tuple(s0i[:, j * LANES:(j + 1) * LANES] for j in range(nsl)))

        def step(k, s):
            s_re, s_im = s
            new_re, new_im = [], []
            for j in range(nsl):
                rows = pl.ds(pl.multiple_of(k * nb, nb), nb)
                b_re = bu_ref.at[j][rows, :]
                b_im = bu_ref.at[nsl + j][rows, :]
                n_re = ar[j] * s_re[j] - ai[j] * s_im[j] + b_re
                n_im = ar[j] * s_im[j] + ai[j] * s_re[j] + b_im
                bu_ref.at[j][rows, :] = n_re
                bu_ref.at[nsl + j][rows, :] = n_im
                new_re.append(n_re)
                new_im.append(n_im)
            return tuple(new_re), tuple(new_im)

        s_re, s_im = lax.fori_loop(0, tt, step, init, unroll=SCAN_UNROLL)
        sr_ref[p] = jnp.concatenate(s_re, axis=-1)
        si_ref[p] = jnp.concatenate(s_im, axis=-1)
        states = jnp.concatenate([bu_ref[j] for j in range(2 * nsl)], axis=-1).astype(bf16)
        y = jnp.dot(states, cblk_ref[p], preferred_element_type=f32) + d_ref[p] * u
        gel_ref[p] = 0.5 * y * (1.0 + lax.erf(y * math.sqrt(0.5)))
        return carry

    lax.fori_loop(0, npack, pack_body, 0)
    for b in range(nb):
        for p in range(npack):
            gout_ref[b, :, p * LANES:(p + 1) * LANES] = gel_ref.at[p][pl.ds(b, tt, stride=nb), :].astype(bf16)


def _s5_scan(x, g, h0_re, h0_im, a_re, a_im, bblk, cblk, dvec, *, tt):
    b, t, d = x.shape
    npack = d // LANES
    sw = S5_PACK * S5_STATE
    rows = b * tt
    vmem = (2 * rows * d * 4 + rows * d * 4 + rows * 2 * sw * 4 + rows * d * 4 + 2 * rows * d * 2
            + 2 * npack * LANES * 2 * sw * 2 + 3 * rows * 2 * sw * 4)
    st_spec = pl.BlockSpec((npack, b, sw), lambda i: (0, 0, 0))
    return pl.pallas_call(
        functools.partial(_s5_kernel, nb=b, tt=tt),
        out_shape=(jax.ShapeDtypeStruct((b, t, d), bf16),
                   jax.ShapeDtypeStruct((npack, b, sw), f32),
                   jax.ShapeDtypeStruct((npack, b, sw), f32)),
        grid=(t // tt,),
        in_specs=[pl.BlockSpec((b, tt, d), lambda i: (0, i, 0)),
                  pl.BlockSpec((1, d), lambda i: (0, 0)),
                  st_spec, st_spec,
                  _resident((npack, 1, sw)), _resident((npack, 1, sw)),
                  _resident((npack, LANES, 2 * sw)), _resident((npack, 2 * sw, LANES)),
                  _resident((npack, 1, LANES))],
        out_specs=(pl.BlockSpec((b, tt, d), lambda i: (0, i, 0)), st_spec, st_spec),
        scratch_shapes=[pltpu.VMEM((npack, rows, LANES), f32),
                        pltpu.VMEM((2 * sw // LANES, rows, LANES), f32),
                        pltpu.VMEM((npack, rows, LANES), f32)],
        compiler_params=_params(("arbitrary",), vmem),
        name="s5_scan",
    )(x, g.reshape(1, d), h0_re, h0_im, a_re, a_im, bblk, cblk, dvec)


def _s5_discretise(a_re, a_im, b_re, b_im, c_re, c_im, log_dt):
    g, n = a_re.shape
    c = b_re.shape[-1]
    npack = g // S5_PACK
    dt = jnp.exp(log_dt)[:, None]
    mag = jnp.exp(dt * a_re)
    abar_re = mag * jnp.cos(dt * a_im)
    abar_im = mag * jnp.sin(dt * a_im)
    den = a_re * a_re + a_im * a_im
    nr = abar_re - 1.0
    f_re = (nr * a_re + abar_im * a_im) / den
    f_im = (abar_im * a_re - nr * a_im) / den
    bbar_re = f_re[..., None] * b_re - f_im[..., None] * b_im
    bbar_im = f_re[..., None] * b_im + f_im[..., None] * b_re
    eye = jnp.eye(S5_PACK, dtype=f32)

    def in_blocks(m):
        m = jnp.swapaxes(m, 1, 2).reshape(npack, S5_PACK, c, n)
        return jnp.einsum('pgcn,gh->pgchn', m, eye).reshape(npack, S5_PACK * c, S5_PACK * n)

    def out_blocks(m):
        m = jnp.swapaxes(m, 1, 2).reshape(npack, S5_PACK, n, c)
        return jnp.einsum('pgnc,gh->pgnhc', m, eye).reshape(npack, S5_PACK * n, S5_PACK * c)

    bblk = jnp.concatenate([in_blocks(bbar_re), in_blocks(bbar_im)], axis=2).astype(bf16)
    cblk = jnp.concatenate([out_blocks(c_re), -out_blocks(c_im)], axis=1).astype(bf16)
    pack = lambda v: v.reshape(npack, 1, S5_PACK * n)
    return pack(abar_re), pack(abar_im), bblk, cblk


def _pack_state(s):
    b, g, n = s.shape
    return jnp.swapaxes(s.reshape(b, g // S5_PACK, S5_PACK * n), 0, 1)


def _unpack_state(s):
    npack, b, sw = s.shape
    return jnp.swapaxes(s, 0, 1).reshape(b, npack * S5_PACK, S5_STATE)


def _qkv_kernel(x_ref, g_ref, w_ref, gq_ref, gk_ref, q_ref, k_ref, v_ref, h_ref, *, nblk):
    n = pl.program_id(1)

    @pl.when(n == 0)
    def _():
        h_ref[...] = _rms(x_ref[...], g_ref[...]).astype(bf16)

    r = jnp.dot(h_ref[...], w_ref[...], preferred_element_type=f32)

    def head_norm(gain):
        heads = [r[:, i * SB_HEAD_DIM:(i + 1) * SB_HEAD_DIM] for i in range(r.shape[1] // SB_HEAD_DIM)]
        return jnp.concatenate([_rms(c, gain) for c in heads], axis=-1)

    @pl.when(n < nblk)
    def _():
        q_ref[...] = head_norm(gq_ref[...]).astype(bf16)

    @pl.when((n >= nblk) & (n < 2 * nblk))
    def _():
        k_ref[...] = head_norm(gk_ref[...])

    @pl.when(n >= 2 * nblk)
    def _():
        v_ref[...] = r


def _qkv_proj(x, g, w_qkv, g_q, g_k, *, tm, tn):
    m, d = x.shape
    nblk = d // tn
    vmem = 2 * tm * d * 4 + tm * d * 2 + 2 * d * tn * 2 + 2 * 3 * tm * tn * 4 + 3 * tm * tn * 4

    def out_spec(which):
        return pl.BlockSpec((tm, tn), lambda i, n: (i, jnp.clip(n - which * nblk, 0, nblk - 1)))

    return pl.pallas_call(
        functools.partial(_qkv_kernel, nblk=nblk),
        out_shape=(jax.ShapeDtypeStruct((m, d), bf16), jax.ShapeDtypeStruct((m, d), f32),
                   jax.ShapeDtypeStruct((m, d), f32)),
        grid=(m // tm, 3 * nblk),
        in_specs=[pl.BlockSpec((tm, d), lambda i, n: (i, 0)),
                  pl.BlockSpec((1, d), lambda i, n: (0, 0)),
                  pl.BlockSpec((d, tn), lambda i, n: (0, n)),
                  pl.BlockSpec((1, SB_HEAD_DIM), lambda i, n: (0, 0)),
                  pl.BlockSpec((1, SB_HEAD_DIM), lambda i, n: (0, 0))],
        out_specs=(out_spec(0), out_spec(1), out_spec(2)),
        scratch_shapes=[pltpu.VMEM((tm, d), bf16)],
        compiler_params=_params(("parallel", "arbitrary"), vmem),
        name="qkv_proj",
    )(x, g.reshape(1, d), w_qkv, g_q.reshape(1, SB_HEAD_DIM), g_k.reshape(1, SB_HEAD_DIM))


def _later_matrix(tk):
    j = lax.broadcasted_iota(jnp.int32, (tk, tk), 0)
    s = lax.broadcasted_iota(jnp.int32, (tk, tk), 1)
    return jnp.where(j > s, 1.0, 0.0).astype(bf16)


def _sb_logw2(z2, later_m, mask):
    chunk = later_m.shape[0]
    l = jnp.log(1.0 + jnp.exp2(-jnp.abs(z2))) * LOG2E
    sp = jnp.maximum(z2, 0.0) + l
    if mask is not None:
        sp = jnp.where(mask, sp, 0.0)
    sp16 = sp.astype(bf16)
    nchunk = z2.shape[1] // chunk
    later, total = [None] * nchunk, None
    for c in range(nchunk - 1, -1, -1):
        cols = slice(c * chunk, (c + 1) * chunk)
        in_chunk = jnp.dot(sp16[:, cols], later_m, preferred_element_type=f32)
        later[c] = in_chunk if total is None else in_chunk + total
        rowsum = jnp.sum(sp[:, cols], axis=-1, keepdims=True)
        total = rowsum if total is None else total + rowsum
    later = later[0] if nchunk == 1 else jnp.concatenate(later, axis=1)
    return jnp.minimum(z2, 0.0) - l - later, total


def _sb_prompt_kernel(bias_ref, q_ref, k_ref, v_ref, o_ref, kb_ref, vb_ref, *, tq):
    kb_ref[...] = k_ref[0].astype(bf16)
    vb_ref[...] = v_ref[0].astype(bf16)
    bias2 = bias_ref[pl.program_id(1)] * LOG2E
    scale2 = LOG2E / math.sqrt(SB_HEAD_DIM)
    later_m = _later_matrix(min(SB_CHUNK, tq))
    r = lax.broadcasted_iota(jnp.int32, (tq, tq), 0)
    c = lax.broadcasted_iota(jnp.int32, (tq, tq), 1)
    causal = c < r
    for qi in range(q_ref.shape[1] // tq):
        q = q_ref[0, qi * tq:(qi + 1) * tq, :]
        acc = jnp.zeros((tq, SB_HEAD_DIM), f32)
        carry = None
        for kb in range(qi, -1, -1):
            rows = slice(kb * tq, (kb + 1) * tq)
            mask = causal if kb == qi else None
            z2 = lax.dot_general(q, kb_ref[rows, :], _NT, preferred_element_type=f32) * scale2 + bias2
            logw, tot = _sb_logw2(z2, later_m, mask)
            if carry is not None:
                logw = logw - carry
            w = jnp.exp2(logw)
            if mask is not None:
                w = jnp.where(mask, w, 0.0)
            acc = acc + jnp.dot(w.astype(bf16), vb_ref[rows, :], preferred_element_type=f32)
            carry = tot if carry is None else carry + tot
        o_ref[0, qi * tq:(qi + 1) * tq, :] = acc.astype(bf16)


def _sb_prompt(q, k, v, bias, *, tq):
    b, t, d = q.shape
    nh = d // SB_HEAD_DIM
    head = pl.BlockSpec((1, t, SB_HEAD_DIM), lambda i, h: (i, 0, h))
    vmem = 2 * 2 * t * SB_HEAD_DIM * 4 + 2 * 2 * t * SB_HEAD_DIM * 2 + 2 * t * SB_HEAD_DIM * 2 + 24 * tq * tq * 4
    return pl.pallas_call(
        functools.partial(_sb_prompt_kernel, tq=tq),
        out_shape=jax.ShapeDtypeStruct((b, t, d), bf16),
        grid=(b, nh),
        in_specs=[pl.BlockSpec(memory_space=pltpu.SMEM), head, head, head],
        out_specs=head,
        scratch_shapes=[pltpu.VMEM((t, SB_HEAD_DIM), bf16), pltpu.VMEM((t, SB_HEAD_DIM), bf16)],
        compiler_params=_params(("parallel", "parallel"), vmem),
        name="sb_prompt",
    )(bias, q, k, v)


def _sb_decode_kernel(pt_ref, q_ref, kn_ref, vn_ref, ck_ref, cv_ref, bias_ref, o_ref,
                      qbd_ref, qstage_ref, acc_ref, carry_ref, k2d_ref, v2d_ref, later_ref,
                      kbuf_ref, vbuf_ref, sem_ref, *, ts, nh, npg, layer, n_pages):
    b = pl.program_id(0)
    i = pl.program_id(1)
    nsteps = pl.num_programs(1) - 1
    nrow = nh * ts
    dh = SB_HEAD_DIM
    scale2 = LOG2E / math.sqrt(dh)

    def page_copies(step, buf):
        copies = []
        for s in range(npg):
            page = pt_ref[b, n_pages - step * npg + s]
            for h in range(nh):
                copies.append(pltpu.make_async_copy(ck_ref.at[layer, page, :, h, :], kbuf_ref.at[buf, s, h],
                                                    sem_ref.at[buf, 0]))
                copies.append(pltpu.make_async_copy(cv_ref.at[layer, page, :, h, :], vbuf_ref.at[buf, s, h],
                                                    sem_ref.at[buf, 1]))
        return copies

    def block(k2d, v2d, later_m, mask):
        z2 = (lax.dot_general(qbd_ref[...], k2d, _NT, preferred_element_type=f32) * scale2
              + bias_ref[...] * LOG2E)
        logw, tot = _sb_logw2(z2, later_m, mask)
        w = jnp.exp2(logw - carry_ref[...])
        if mask is not None:
            w = jnp.where(mask, w, 0.0)
        carry_ref[...] += tot
        out = jnp.dot(w.astype(bf16), v2d, preferred_element_type=f32)
        for h in range(nh):
            acc_ref[h * ts:(h + 1) * ts, :] += out[h * ts:(h + 1) * ts, h * dh:(h + 1) * dh]

    @pl.when(i == 0)
    def _():
        for c in page_copies(1, 1):
            c.start()
        qstage_ref[...] = jnp.zeros_like(qstage_ref)
        for h in range(nh):
            qstage_ref[h * ts:(h + 1) * ts, h * dh:(h + 1) * dh] = q_ref[0, :, h * dh:(h + 1) * dh]
        qbd_ref[...] = qstage_ref[...].astype(bf16)
        acc_ref[...] = jnp.zeros_like(acc_ref)
        carry_ref[...] = jnp.zeros_like(carry_ref)
        later_ref[...] = _later_matrix(later_ref.shape[0])
        nk = kn_ref.shape[1]
        t_of_row = lax.broadcasted_iota(jnp.int32, (nrow, nk), 0) % ts
        j = lax.broadcasted_iota(jnp.int32, (nrow, nk), 1)
        chunk = min(nk, later_ref.shape[0])
        block(kn_ref[0].astype(bf16), vn_ref[0].astype(bf16), later_ref[0:chunk, 0:chunk], j < t_of_row)

    def page_step(buf):
        @pl.when(i < nsteps)
        def _():
            for c in page_copies(i + 1, 1 - buf):
                c.start()
        for c in page_copies(i, buf):
            c.wait()
        for s in range(npg):
            for h in range(nh):
                keys = slice(s * PAGE_SIZE, (s + 1) * PAGE_SIZE)
                k2d_ref[keys, h * dh:(h + 1) * dh] = kbuf_ref[buf, s, h].astype(bf16)
                v2d_ref[keys, h * dh:(h + 1) * dh] = vbuf_ref[buf, s, h].astype(bf16)
        block(k2d_ref[...], v2d_ref[...], later_ref[...], None)

    @pl.when((i > 0) & (i % 2 == 1))
    def _():
        page_step(1)

    @pl.when((i > 0) & (i % 2 == 0))
    def _():
        page_step(0)

    @pl.when(i == nsteps)
    def _():
        for h in range(nh):
            o_ref[0, :, h * dh:(h + 1) * dh] = acc_ref[h * ts:(h + 1) * ts, :]


def _sb_decode(q, k_new, v_new, cache_k, cache_v, page_table, bias_rows, *, npg, layer):
    b, ts, d = q.shape
    nh = d // SB_HEAD_DIM
    n_pages = page_table.shape[1]
    nrow = nh * ts
    nkeys = npg * PAGE_SIZE
    chunk = min(SB_CHUNK, nkeys)
    per_b = lambda i, j, pt: (i, 0, 0)
    slabs = (2, npg, nh, PAGE_SIZE, SB_HEAD_DIM)
    vmem = (2 * 2 * npg * PAGE_SIZE * d * 4 + 2 * 2 * PAGE_SIZE * d * 4 + 2 * nkeys * d * 2
            + nrow * d * 6 + chunk * chunk * 2 + 4 * nrow * d * 4)
    return pl.pallas_call(
        functools.partial(_sb_decode_kernel, ts=ts, nh=nh, npg=npg, layer=layer, n_pages=n_pages),
        out_shape=jax.ShapeDtypeStruct((b, ts, d), f32),
        grid_spec=pltpu.PrefetchScalarGridSpec(
            num_scalar_prefetch=1,
            grid=(b, n_pages // npg + 1),
            in_specs=[pl.BlockSpec((1, ts, d), per_b),
                      pl.BlockSpec((1, PAGE_SIZE, d), per_b),
                      pl.BlockSpec((1, PAGE_SIZE, d), per_b),
                      pl.BlockSpec(memory_space=pl.ANY),
                      pl.BlockSpec(memory_space=pl.ANY),
                      pl.BlockSpec((nrow, 1), lambda i, j, pt: (0, 0))],
            out_specs=pl.BlockSpec((1, ts, d), per_b),
            scratch_shapes=[pltpu.VMEM((nrow, d), bf16), pltpu.VMEM((nrow, d), f32),
                            pltpu.VMEM((nrow, SB_HEAD_DIM), f32), pltpu.VMEM((nrow, 1), f32),
                            pltpu.VMEM((nkeys, d), bf16), pltpu.VMEM((nkeys, d), bf16),
                            pltpu.VMEM((chunk, chunk), bf16),
                            pltpu.VMEM(slabs, f32), pltpu.VMEM(slabs, f32),
                            pltpu.SemaphoreType.DMA((2, 2))]),
        compiler_params=_params(("arbitrary", "arbitrary"), vmem),
        name="sb_decode",
    )(page_table, q, k_new, v_new, cache_k, cache_v, bias_rows)


def kernel(x_prompt, x_sample, cache_pool, state_ssm_re, state_ssm_im, cache_k, cache_v, page_table,
           norm_mix, norm_ffn, w_ffn_gate, w_ffn_up, w_ffn_down, w_pool, pool_scale,
           ssm_a_re, ssm_a_im, ssm_b_re, ssm_b_im, ssm_c_re, ssm_c_im, ssm_d, ssm_log_dt,
           w_glu_a, w_glu_b, w_qkv, w_o, sb_q_norm, sb_k_norm, sb_bias):
    bp, tp, d = x_prompt.shape
    bs, ts, _ = x_sample.shape
    depth = norm_mix.shape[0]
    nh = d // SB_HEAD_DIM
    mp, ms = bp * tp, bs * ts
    past_len = page_table.shape[1] * PAGE_SIZE
    n_phys = cache_k.shape[1]
    tm = TM

    xp, xs = x_prompt, x_sample
    wg, wu, wd = w_ffn_gate.astype(bf16), w_ffn_up.astype(bf16), w_ffn_down.astype(bf16)
    pool_p, pool_s = [], []
    ssm_p, ssm_s = [], []
    kv_p, kv_s = [], []
    for i in range(depth):
        kind = i % N_MIXERS
        j = i // N_MIXERS
        if kind == 0:
            w = w_pool[j].astype(bf16)
            zero_prev = jnp.zeros((bp, POOL_HALO, d), f32)
            prev = jnp.pad(cache_pool[j], ((0, 0), (1, 0), (0, 0)))
            xp, buf_p = _pool_layer(xp, zero_prev, norm_mix[i], w, pool_scale[j], tt=POOL_TT, start_pos=0)
            xs, buf_s = _pool_layer(xs, prev, norm_mix[i], w, pool_scale[j], tt=ts, start_pos=past_len)
            pool_p.append(buf_p[:, 1:])
            pool_s.append(buf_s[:, 1:])
        elif kind == 1:
            a_re, a_im, bblk, cblk = _s5_discretise(ssm_a_re[j], ssm_a_im[j], ssm_b_re[j], ssm_b_im[j],
                                                    ssm_c_re[j], ssm_c_im[j], ssm_log_dt[j])
            dvec = ssm_d[j].reshape(d // LANES, 1, LANES)
            zero_state = jnp.zeros((d // LANES, bp, S5_PACK * S5_STATE), f32)
            gp, sr_p, si_p = _s5_scan(xp, norm_mix[i], zero_state, zero_state, a_re, a_im, bblk, cblk, dvec,
                                      tt=S5_TT)
            gs, sr_s, si_s = _s5_scan(xs, norm_mix[i], _pack_state(state_ssm_re[j]), _pack_state(state_ssm_im[j]),
                                      a_re, a_im, bblk, cblk, dvec, tt=ts)
            wa, wb = w_glu_a[j].astype(bf16), w_glu_b[j].astype(bf16)
            xp = _glu_residual(xp.reshape(mp, d), gp.reshape(mp, d), wa, wb, tm=TM_GLU).reshape(bp, tp, d)
            xs = _glu_residual(xs.reshape(ms, d), gs.reshape(ms, d), wa, wb, tm=ms).reshape(bs, ts, d)
            ssm_p.append((_unpack_state(sr_p), _unpack_state(si_p)))
            ssm_s.append((_unpack_state(sr_s), _unpack_state(si_s)))
        else:
            wqkv, wo = w_qkv[j].astype(bf16), w_o[j].astype(bf16)
            q_p, k_p, v_p = _qkv_proj(xp.reshape(mp, d), norm_mix[i], wqkv, sb_q_norm[j], sb_k_norm[j],
                                      tm=TM_QKV, tn=TN_QKV)
            q_s, k_s, v_s = _qkv_proj(xs.reshape(ms, d), norm_mix[i], wqkv, sb_q_norm[j], sb_k_norm[j],
                                      tm=ms, tn=TN_QKV)
            o_p = _sb_prompt(q_p.reshape(bp, tp, d), k_p.reshape(bp, tp, d), v_p.reshape(bp, tp, d), sb_bias[j],
                             tq=TQ)
            pad_new = lambda a: jnp.pad(a.reshape(bs, ts, d), ((0, 0), (0, PAGE_SIZE - ts), (0, 0)))
            o_s = _sb_decode(q_s.reshape(bs, ts, d).astype(f32), pad_new(k_s), pad_new(v_s),
                             cache_k, cache_v, page_table, jnp.repeat(sb_bias[j], ts).reshape(nh * ts, 1),
                             npg=DECODE_PAGES, layer=j)
            xp = _proj_residual(xp.reshape(mp, d), o_p.reshape(mp, d), wo, tm=tm).reshape(bp, tp, d)
            xs = _proj_residual(xs.reshape(ms, d), o_s.reshape(ms, d).astype(bf16), wo, tm=ms).reshape(bs, ts, d)
            kv_p.append((k_p.reshape(bp, tp, nh, SB_HEAD_DIM), v_p.reshape(bp, tp, nh, SB_HEAD_DIM)))
            kv_s.append((k_s.reshape(bs, ts, nh, SB_HEAD_DIM), v_s.reshape(bs, ts, nh, SB_HEAD_DIM)))
        xp = _ffn(xp.reshape(mp, d), norm_ffn[i], wg, wu, wd, i, tm=TM_FFN).reshape(bp, tp, d)
        xs = _ffn(xs.reshape(ms, d), norm_ffn[i], wg, wu, wd, i, tm=ms).reshape(bs, ts, d)

    stack = lambda items: jnp.stack(items, axis=0)
    return (xp, xs, stack(pool_p), stack(pool_s),
            stack([s[0] for s in ssm_p]), stack([s[1] for s in ssm_p]),
            stack([s[0] for s in ssm_s]), stack([s[1] for s in ssm_s]),
            stack([kv[0] for kv in kv_p]), stack([kv[1] for kv in kv_p]),
            stack([kv[0] for kv in kv_s]), stack([kv[1] for kv in kv_s]))
```
